```python
import jax, jax.numpy as jnp
from jax import lax
import numpy as np

D_MODEL = 2048
BATCH = 8
SEQ = 2048
DEPTH = 2

MEM_LEN = 256
D_MIX = D_MODEL
DN_HEADS = 8
DN_HEAD_DIM = 128
DN_WIDTH = DN_HEADS * DN_HEAD_DIM
POOL_WINDOWS = (2, 4, 8, 16)
N_POOL_GROUPS = len(POOL_WINDOWS)
POOL_WIDTH = D_MIX - DN_WIDTH
POOL_GROUP_DIM = POOL_WIDTH // N_POOL_GROUPS
CONV_WIDTH = 4
CHUNK = 64
D_FF = ((8 * D_MODEL // 3 + 255) // 256) * 256
X_HEADS = 4
X_HEAD_DIM = D_MODEL // X_HEADS
EPS = 1e-6
IN_COLS = 4 * DN_WIDTH + 2 * DN_HEADS + POOL_WIDTH

kernel_name = 'hybrid_deltanet_pool_macaron'


def rmsnorm(x, g):
    xf = x.astype(jnp.float32)
    y = xf * lax.rsqrt(jnp.mean(xf * xf, axis=-1, keepdims=True) + EPS)
    return (y * g.astype(jnp.float32)).astype(x.dtype)


def l2norm(x):
    return x * lax.rsqrt(jnp.sum(x * x, axis=-1, keepdims=True) + EPS)


def swiglu(x, w_gate, w_up, w_down):
    return (jax.nn.silu(x @ w_gate) * (x @ w_up)) @ w_down


def causal_depthwise_conv(x, w):
    c = x.shape[-1]
    return lax.conv_general_dilated(
        x, w[:, None, :], window_strides=(1,), padding=[(CONV_WIDTH - 1, 0)],
        dimension_numbers=('NWC', 'WIO', 'NWC'), feature_group_count=c)


def gated_delta_rule(q, k, v, g, beta):
    b_, s_, h_, dk = q.shape
    dv = v.shape[-1]
    n_chunks = s_ // CHUNK

    def chunks(t):
        t = t.reshape((b_, n_chunks, CHUNK, h_) + t.shape[3:])
        return jnp.moveaxis(t, 3, 1)

    q, k, v, g, beta = chunks(q), chunks(k), chunks(v), chunks(g), chunks(beta)
    gc = jnp.cumsum(g, axis=-1)
    causal = jnp.tril(jnp.ones((CHUNK, CHUNK), dtype=bool))
    strict = jnp.tril(jnp.ones((CHUNK, CHUNK), dtype=bool), -1)
    decay_mat = jnp.exp(jnp.where(causal, gc[..., :, None] - gc[..., None, :], -jnp.inf))
    kb = k * beta[..., None]
    a_mat = jnp.where(strict, jnp.einsum('bhnid,bhnjd->bhnij', kb, k) * decay_mat, 0.0)
    rhs = jnp.concatenate([v * beta[..., None], kb * jnp.exp(gc)[..., None]], axis=-1)
    sol = lax.linalg.triangular_solve(a_mat, rhs, left_side=True, lower=True, unit_diagonal=True)
    u, w = sol[..., :dv], sol[..., dv:]
    qk = jnp.einsum('bhnid,bhnjd->bhnij', q, k) * decay_mat
    q_dec = q * jnp.exp(gc)[..., None]
    k_dec = k * jnp.exp(gc[..., -1:] - gc)[..., None]
    g_last = jnp.exp(gc[..., -1])

    def step(state, inp):
        u_i, w_i, qk_i, qd_i, kd_i, gl_i = inp
        v_new = u_i - jnp.einsum('bhck,bhkv->bhcv', w_i, state)
        o_i = jnp.einsum('bhck,bhkv->bhcv', qd_i, state) + jnp.einsum('bhcj,bhjv->bhcv', qk_i, v_new)
        state = state * gl_i[..., None, None] + jnp.einsum('bhck,bhcv->bhkv', kd_i, v_new)
        return state, o_i

    xs = tuple(jnp.moveaxis(t, 2, 0) for t in (u, w, qk, q_dec, k_dec, g_last))
    s0 = jnp.zeros((b_, h_, dk, dv), jnp.float32)
    _, o = lax.scan(step, s0, xs)
    return jnp.transpose(o, (1, 0, 3, 2, 4)).reshape(b_, s_, h_, dv)


def deltanet_branch(qkv, z, a, b, conv_w, a_log, dt_bias, out_norm_g):
    bsz, s_, _ = qkv.shape
    qkv = jax.nn.silu(causal_depthwise_conv(qkv, conv_w)).astype(jnp.float32)
    q, k, v = jnp.split(qkv, 3, axis=-1)
    shp = (bsz, s_, DN_HEADS, DN_HEAD_DIM)
    q = l2norm(q.reshape(shp)) * (DN_HEAD_DIM ** -0.5)
    k = l2norm(k.reshape(shp))
    v = v.reshape(shp)
    g = -jnp.exp(a_log.astype(jnp.float32)) * jax.nn.softplus(a.astype(jnp.float32) + dt_bias.astype(jnp.float32))
    beta = jax.nn.sigmoid(b.astype(jnp.float32))
    o = gated_delta_rule(q, k, v, g, beta)
    o = rmsnorm(o, out_norm_g) * jax.nn.silu(z.reshape(shp).astype(jnp.float32))
    return o.reshape(bsz, s_, DN_WIDTH).astype(z.dtype)


def pool_branch(p, pool_w, pool_scale):
    bsz, s_, _ = p.shape
    pf = p.astype(jnp.float32).reshape(bsz, s_, N_POOL_GROUPS, POOL_GROUP_DIM)
    csum = jnp.concatenate([jnp.zeros((bsz, 1, N_POOL_GROUPS, POOL_GROUP_DIM), jnp.float32),
                            jnp.cumsum(pf, axis=1)], axis=1)
    pos1 = jnp.arange(1, s_ + 1)
    outs = []
    for gi, win in enumerate(POOL_WINDOWS):
        c_g = csum[:, :, gi]
        lower = jnp.concatenate([jnp.zeros((bsz, win - 1, POOL_GROUP_DIM), jnp.float32),
                                 c_g[:, :s_ + 1 - win]], axis=1)
        count = jnp.minimum(pos1, win).astype(jnp.float32)[None, :, None]
        outs.append((c_g[:, 1:] - lower) / count)
    pooled = jnp.stack(outs, axis=2) - pf
    mixed = jnp.einsum('bsgc,gcd->bsgd', pooled.astype(p.dtype), pool_w)
    return mixed.reshape(bsz, s_, POOL_WIDTH) * pool_scale


def memory_cross_attention(h, mem_n, wq, wkv, wo):
    bsz, s_, _ = h.shape
    m_ = mem_n.shape[1]
    q = (h @ wq).reshape(bsz, s_, X_HEADS, X_HEAD_DIM)
    k, v = jnp.split(mem_n @ wkv, 2, axis=-1)
    k = k.reshape(bsz, m_, X_HEADS, X_HEAD_DIM)
    v = v.reshape(bsz, m_, X_HEADS, X_HEAD_DIM)
    s = jnp.einsum('bshd,bmhd->bhsm', q, k).astype(jnp.float32) * (X_HEAD_DIM ** -0.5)
    p = jax.nn.softmax(s, axis=-1).astype(v.dtype)
    o = jnp.einsum('bhsm,bmhd->bshd', p, v).reshape(bsz, s_, D_MODEL)
    return o @ wo


def setup_inputs(seed: int = 0) -> dict:
    key = jax.random.key(seed)
    ks = jax.random.split(key, 26)
    f32 = jnp.float32

    def dense(k, fan_in, fan_out):
        return jax.random.normal(k, (DEPTH, fan_in, fan_out), f32) * fan_in ** -0.5

    def gain(k, n, lead=(DEPTH,)):
        return 1.0 + 0.01 * jax.random.normal(k, lead + (n,), f32)

    dt = jnp.exp(jax.random.uniform(ks[8], (DEPTH, DN_HEADS), f32, np.log(1e-3), np.log(1e-1)))
    return {
        'x': jax.random.normal(ks[0], (BATCH, SEQ, D_MODEL), f32),
        'mem': jax.random.normal(ks[1], (BATCH, MEM_LEN, D_MODEL), f32),
        'ffn1_norm': gain(ks[2], D_MODEL),
        'ffn1_w_gate': dense(ks[3], D_MODEL, D_FF),
        'ffn1_w_up': dense(ks[4], D_MODEL, D_FF),
        'ffn1_w_down': dense(ks[5], D_FF, D_MODEL),
        'mix_norm': gain(ks[6], D_MODEL),
        'w_in': dense(ks[7], D_MODEL, IN_COLS),
        'conv_w': jax.random.normal(ks[9], (DEPTH, CONV_WIDTH, 3 * DN_WIDTH), f32) * CONV_WIDTH ** -0.5,
        'a_log': jnp.log(jax.random.uniform(ks[10], (DEPTH, DN_HEADS), f32, 1.0, 16.0)),
        'dt_bias': dt + jnp.log(-jnp.expm1(-dt)),
        'dn_out_norm': gain(ks[11], DN_HEAD_DIM),
        'pool_w': jax.random.normal(ks[12], (DEPTH, N_POOL_GROUPS, POOL_GROUP_DIM, POOL_GROUP_DIM), f32) * POOL_GROUP_DIM ** -0.5,
        'pool_scale': 1.0 + 0.05 * jax.random.normal(ks[13], (DEPTH, POOL_WIDTH), f32),
        'w_out': dense(ks[14], D_MIX, D_MODEL),
        'xattn_norm': gain(ks[15], D_MODEL),
        'mem_norm': gain(ks[16], D_MODEL),
        'xattn_wq': dense(ks[17], D_MODEL, D_MODEL),
        'xattn_wkv': dense(ks[18], D_MODEL, 2 * D_MODEL),
        'xattn_wo': dense(ks[19], D_MODEL, D_MODEL),
        'ffn2_norm': gain(ks[20], D_MODEL),
        'ffn2_w_gate': dense(ks[21], D_MODEL, D_FF),
        'ffn2_w_up': dense(ks[22], D_MODEL, D_FF),
        'ffn2_w_down': dense(ks[23], D_FF, D_MODEL),
        'final_norm': gain(ks[24], D_MODEL, lead=()),
    }


def reference(x, mem, ffn1_norm, ffn1_w_gate, ffn1_w_up, ffn1_w_down, mix_norm, w_in, conv_w,
              a_log, dt_bias, dn_out_norm, pool_w, pool_scale, w_out, xattn_norm, mem_norm,
              xattn_wq, xattn_wkv, xattn_wo, ffn2_norm, ffn2_w_gate, ffn2_w_up, ffn2_w_down,
              final_norm):
    o_q = 0
    o_z = 3 * DN_WIDTH
    o_a = 4 * DN_WIDTH
    o_b = o_a + DN_HEADS
    o_p = o_b + DN_HEADS
    for l in range(DEPTH):
        h = x + 0.5 * swiglu(rmsnorm(x, ffn1_norm[l]), ffn1_w_gate[l], ffn1_w_up[l], ffn1_w_down[l])
        proj = rmsnorm(h, mix_norm[l]) @ w_in[l]
        y_dn = deltanet_branch(proj[..., o_q:o_z], proj[..., o_z:o_a], proj[..., o_a:o_b],
                               proj[..., o_b:o_p], conv_w[l], a_log[l], dt_bias[l], dn_out_norm[l])
        y_pool = pool_branch(proj[..., o_p:], pool_w[l], pool_scale[l])
        h = h + jnp.concatenate([y_dn, y_pool], axis=-1) @ w_out[l]
        h = h + memory_cross_attention(rmsnorm(h, xattn_norm[l]), rmsnorm(mem, mem_norm[l]),
                                       xattn_wq[l], xattn_wkv[l], xattn_wo[l])
        x = h + 0.5 * swiglu(rmsnorm(h, ffn2_norm[l]), ffn2_w_gate[l], ffn2_w_up[l], ffn2_w_down[l])
    return rmsnorm(x, final_norm)
```

```python
import functools

import jax
import jax.numpy as jnp
from jax import lax
from jax.experimental import pallas as pl
from jax.experimental.pallas import tpu as pltpu

F32 = jnp.float32
BF16 = jnp.bfloat16

D_MODEL = 2048
MEM_LEN = 256
DN_HEADS = 8
DN_HEAD_DIM = 128
DN_WIDTH = DN_HEADS * DN_HEAD_DIM
POOL_WINDOWS = (2, 4, 8, 16)
POOL_WIDTH = D_MODEL - DN_WIDTH
POOL_GROUP_DIM = POOL_WIDTH // len(POOL_WINDOWS)
POOL_HALO = 16
CONV_WIDTH = 4
CONV_HALO = 8
CHUNK = 64
X_HEADS = 4
X_HEAD_DIM = D_MODEL // X_HEADS
EPS = 1e-6
LANES = 128
VMEM_LIMIT = 56 * 1024 * 1024


def _cparams(*sem):
    return pltpu.CompilerParams(dimension_semantics=sem, vmem_limit_bytes=VMEM_LIMIT)


def _rmsnorm(x, g):
    ms = jnp.mean(x * x, axis=-1, keepdims=True)
    return x * lax.rsqrt(ms + EPS) * g


def _sigmoid(x):
    return 1.0 / (1.0 + jnp.exp(-x))


def _dot(a, b):
    return jnp.dot(a, b, preferred_element_type=F32)


def _dot_nt(a, b):
    return lax.dot_general(a, b, (((1,), (1,)), ((), ())), preferred_element_type=F32)


def _dot_tn(a, b):
    return lax.dot_general(a, b, (((0,), (0,)), ((), ())), preferred_element_type=F32)


def _split2(x):
    hi = x.astype(BF16)
    lo = (x - hi.astype(F32)).astype(BF16)
    return hi, lo


def _split3(x):
    hi = x.astype(BF16)
    r1 = x - hi.astype(F32)
    mid = r1.astype(BF16)
    lo = (r1 - mid.astype(F32)).astype(BF16)
    return hi, mid, lo


def _dot_hp(a, b):
    ah, al = _split2(a)
    bh, bl = _split2(b)
    return _dot(ah, bh) + (_dot(ah, bl) + _dot(al, bh))


def _ffn_kernel(x_ref, g_ref, wg_ref, wu_ref, wd_ref, fg_ref, o_ref, xn_ref, acc_ref, *, final_norm):
    j = pl.program_id(1)

    @pl.when(j == 0)
    def _():
        xn_ref[...] = _rmsnorm(x_ref[...], g_ref[...]).astype(BF16)

    xn = xn_ref[...]
    gate = _dot(xn, wg_ref[...])
    up = _dot(xn, wu_ref[...])
    hid = (gate * _sigmoid(gate) * up).astype(BF16)
    part = _dot(hid, wd_ref[...])

    @pl.when(j == 0)
    def _():
        acc_ref[...] = part

    @pl.when(j > 0)
    def _():
        acc_ref[...] += part

    @pl.when(j == pl.num_programs(1) - 1)
    def _():
        y = x_ref[...] + 0.5 * acc_ref[...]
        if final_norm:
            y = _rmsnorm(y, fg_ref[...])
        o_ref[...] = y


def _ffn(x, g, wg, wu, wd, fg, *, final_norm, tm=512, tf=512):
    t, d = x.shape
    dff = wg.shape[1]
    return pl.pallas_call(
        functools.partial(_ffn_kernel, final_norm=final_norm),
        grid=(t // tm, dff // tf),
        in_specs=[
            pl.BlockSpec((tm, d), lambda i, j: (i, 0)),
            pl.BlockSpec((1, d), lambda i, j: (0, 0)),
            pl.BlockSpec((d, tf), lambda i, j: (0, j)),
            pl.BlockSpec((d, tf), lambda i, j: (0, j)),
            pl.BlockSpec((tf, d), lambda i, j: (j, 0)),
            pl.BlockSpec((1, d), lambda i, j: (0, 0)),
        ],
        out_specs=pl.BlockSpec((tm, d), lambda i, j: (i, 0)),
        out_shape=jax.ShapeDtypeStruct((t, d), F32),
        scratch_shapes=[pltpu.VMEM((tm, d), BF16), pltpu.VMEM((tm, d), F32)],
        compiler_params=_cparams("parallel", "arbitrary"),
        name="ffn",
    )(x, g, wg, wu, wd, fg)


def _nmm_kernel(x_ref, g_ref, w_ref, o_ref, xn_ref):
    @pl.when(pl.program_id(1) == 0)
    def _():
        xn_ref[...] = _rmsnorm(x_ref[...], g_ref[...]).astype(BF16)

    o_ref[...] = _dot(xn_ref[...], w_ref[...]).astype(o_ref.dtype)


def _norm_matmul(x, g, w, out_dtype, *, tm, tn):
    t, d = x.shape
    n = w.shape[1]
    return pl.pallas_call(
        _nmm_kernel,
        grid=(t // tm, n // tn),
        in_specs=[
            pl.BlockSpec((tm, d), lambda i, j: (i, 0)),
            pl.BlockSpec((1, d), lambda i, j: (0, 0)),
            pl.BlockSpec((d, tn), lambda i, j: (0, j)),
        ],
        out_specs=pl.BlockSpec((tm, tn), lambda i, j: (i, j)),
        out_shape=jax.ShapeDtypeStruct((t, n), out_dtype),
        scratch_shapes=[pltpu.VMEM((tm, d), BF16)],
        compiler_params=_cparams("parallel", "arbitrary"),
        name="norm_matmul",
    )(x, g, w)


def _inproj_kernel(x_ref, g_ref, w_ref, wab_ref, o_ref, ab_ref, xn_ref):
    @pl.when(pl.program_id(1) == 0)
    def _():
        xn = _rmsnorm(x_ref[...], g_ref[...]).astype(BF16)
        xn_ref[...] = xn
        ab_ref[...] = _dot(xn, wab_ref[...])

    o_ref[...] = _dot(xn_ref[...], w_ref[...])


def _inproj(x, g, w, wab, *, tm=1024, tn=512):
    t, d = x.shape
    n = w.shape[1]
    nab = wab.shape[1]
    return pl.pallas_call(
        _inproj_kernel,
        grid=(t // tm, n // tn),
        in_specs=[
            pl.BlockSpec((tm, d), lambda i, j: (i, 0)),
            pl.BlockSpec((1, d), lambda i, j: (0, 0)),
            pl.BlockSpec((d, tn), lambda i, j: (0, j)),
            pl.BlockSpec((d, nab), lambda i, j: (0, 0)),
        ],
        out_specs=[
            pl.BlockSpec((tm, tn), lambda i, j: (i, j)),
            pl.BlockSpec((tm, nab), lambda i, j: (i, 0)),
        ],
        out_shape=[jax.ShapeDtypeStruct((t, n), F32), jax.ShapeDtypeStruct((t, nab), F32)],
        scratch_shapes=[pltpu.VMEM((tm, d), BF16)],
        compiler_params=_cparams("parallel", "arbitrary"),
        name="inproj",
    )(x, g, w, wab)


def _dn_kernel(qkv_ref, z_ref, ab_ref, cw_ref, alog_ref, dtb_ref, on_ref, y_ref,
               xbuf, act, gc_s, gct_s, beta_s, state, *, tc):
    s = pl.program_id(1)
    nchunk = tc // CHUNK
    qkv_w = 3 * DN_WIDTH

    @pl.when(s == 0)
    def _():
        xbuf[0:CONV_HALO, :] = jnp.zeros((CONV_HALO, qkv_w), F32)
        state[...] = jnp.zeros_like(state)

    xbuf[CONV_HALO:CONV_HALO + tc, :] = qkv_ref[0]

    def conv_block(cb, carry):
        col = pl.ds(pl.multiple_of(cb * LANES, LANES), LANES)
        w = cw_ref[:, col]
        acc = xbuf[CONV_HALO:CONV_HALO + tc, col] * w[3:4, :]
        for j in range(CONV_WIDTH - 1):
            off = CONV_HALO - (CONV_WIDTH - 1) + j
            acc = acc + xbuf[off:off + tc, col] * w[j:j + 1, :]
        y = acc * _sigmoid(acc)
        nrm = lax.rsqrt(jnp.sum(y * y, axis=-1, keepdims=True) + EPS)
        mult = jnp.where(cb < DN_HEADS, nrm * (DN_HEAD_DIM ** -0.5),
                         jnp.where(cb < 2 * DN_HEADS, nrm, 1.0))
        act[:, col] = y * mult
        return carry

    lax.fori_loop(0, qkv_w // LANES, conv_block, 0)
    xbuf[0:CONV_HALO, :] = xbuf[tc:tc + CONV_HALO, :]

    ab = ab_ref[0]
    xg = ab[:, 0:LANES] + dtb_ref[...]
    softplus = jnp.maximum(xg, 0.0) + jnp.log1p(jnp.exp(-jnp.abs(xg)))
    g = -jnp.exp(alog_ref[...]) * softplus
    beta_s[...] = _sigmoid(ab[:, LANES:2 * LANES])

    ri = lax.broadcasted_iota(jnp.int32, (tc, tc), 0)
    ci = lax.broadcasted_iota(jnp.int32, (tc, tc), 1)
    same_chunk = jnp.right_shift(ri, 6) == jnp.right_shift(ci, 6)
    tri = jnp.where(same_chunk, jnp.where(ci <= ri, 1.0, 0.0), 0.0).astype(BF16)
    gh, gm, gl = _split3(g)
    gc = _dot(tri, gh) + _dot(tri, gm) + _dot(tri, gl)
    gc_s[...] = gc

    er = lax.broadcasted_iota(jnp.int32, (LANES, LANES), 0)
    ec = lax.broadcasted_iota(jnp.int32, (LANES, LANES), 1)
    eye = jnp.where(er == ec, 1.0, 0.0).astype(BF16)
    for c in range(nchunk):
        ch, cm, cl = _split3(gc[c * CHUNK:(c + 1) * CHUNK, :])
        t = _dot_nt(eye, ch) + _dot_nt(eye, cm) + _dot_nt(eye, cl)
        gct_s[c] = t[0:DN_HEADS, :]

    r64 = lax.broadcasted_iota(jnp.int32, (CHUNK, CHUNK), 0)
    c64 = lax.broadcasted_iota(jnp.int32, (CHUNK, CHUNK), 1)
    causal = r64 >= c64
    strict = r64 > c64
    eye64 = jnp.where(r64 == c64, 1.0, 0.0)
    onorm = on_ref[...]

    def chunk_body(c, carry):
        rows = pl.ds(pl.multiple_of(c * CHUNK, CHUNK), CHUNK)
        gcc = gc_s[rows, :]
        gct = gct_s[c]
        bet = beta_s[rows, :]
        egc = jnp.exp(gcc)
        glast = gcc[CHUNK - 1:CHUNK, :]
        ekd = jnp.exp(glast - gcc)
        egl = jnp.exp(glast)
        for h in range(DN_HEADS):
            hc = slice(h * DN_HEAD_DIM, (h + 1) * DN_HEAD_DIM)
            q = act[rows, hc]
            k = act[rows, DN_WIDTH + h * DN_HEAD_DIM:DN_WIDTH + (h + 1) * DN_HEAD_DIM]
            v = act[rows, 2 * DN_WIDTH + h * DN_HEAD_DIM:2 * DN_WIDTH + (h + 1) * DN_HEAD_DIM]
            bcol = bet[:, h:h + 1]
            kb = k * bcol
            k16 = k.astype(BF16)
            dec = jnp.where(causal, jnp.exp(gcc[:, h:h + 1] - gct[h:h + 1, :]), 0.0)
            a_mat = jnp.where(strict, _dot_nt(kb.astype(BF16), k16) * dec, 0.0)
            tinv = eye64 - a_mat
            apow = a_mat
            for _ in range(5):
                apow = _dot_hp(apow, apow)
                tinv = tinv + _dot_hp(tinv, apow)
            rhs = jnp.concatenate([v * bcol, kb * egc[:, h:h + 1]], axis=1)
            uw = _dot_hp(tinv, rhs)
            u = uw[:, 0:DN_HEAD_DIM]
            w = uw[:, DN_HEAD_DIM:2 * DN_HEAD_DIM]
            qk = _dot_nt(q.astype(BF16), k16) * dec
            st = state[h]
            st16 = st.astype(BF16)
            v_new = u - _dot(w.astype(BF16), st16)
            vn16 = v_new.astype(BF16)
            o = _dot((q * egc[:, h:h + 1]).astype(BF16), st16) + _dot(qk.astype(BF16), vn16)
            kd = (k * ekd[:, h:h + 1]).astype(BF16)
            state[h] = st * egl[:, h:h + 1] + _dot_tn(kd, vn16)
            zh = z_ref[0, rows, hc]
            y = _rmsnorm(o, onorm) * (zh * _sigmoid(zh))
            y_ref[0, rows, hc] = y.astype(y_ref.dtype)
        return carry

    lax.fori_loop(0, nchunk, chunk_body, 0)


def _deltanet(proj, ab, conv_w, alog, dtb, onorm, *, tc=256):
    b, s, _ = proj.shape
    qkv_w = 3 * DN_WIDTH
    return pl.pallas_call(
        functools.partial(_dn_kernel, tc=tc),
        grid=(b, s // tc),
        in_specs=[
            pl.BlockSpec((1, tc, qkv_w), lambda i, j: (i, j, 0)),
            pl.BlockSpec((1, tc, DN_WIDTH), lambda i, j: (i, j, qkv_w // DN_WIDTH)),
            pl.BlockSpec((1, tc, 2 * LANES), lambda i, j: (i, j, 0)),
            pl.BlockSpec((CONV_WIDTH, qkv_w), lambda i, j: (0, 0)),
            pl.BlockSpec((1, LANES), lambda i, j: (0, 0)),
            pl.BlockSpec((1, LANES), lambda i, j: (0, 0)),
            pl.BlockSpec((1, DN_HEAD_DIM), lambda i, j: (0, 0)),
        ],
        out_specs=pl.BlockSpec((1, tc, DN_WIDTH), lambda i, j: (i, j, 0)),
        out_shape=jax.ShapeDtypeStruct((b, s, DN_WIDTH), BF16),
        scratch_shapes=[
            pltpu.VMEM((tc + CONV_HALO, qkv_w), F32),
            pltpu.VMEM((tc, qkv_w), F32),
            pltpu.VMEM((tc, LANES), F32),
            pltpu.VMEM((tc // CHUNK, DN_HEADS, CHUNK), F32),
            pltpu.VMEM((tc, LANES), F32),
            pltpu.VMEM((DN_HEADS, DN_HEAD_DIM, DN_HEAD_DIM), F32),
        ],
        compiler_params=_cparams("parallel", "arbitrary"),
        name="deltanet",
    )(proj, proj, ab, conv_w, alog, dtb, onorm)


def _pool_kernel(p_ref, pw_ref, ps_ref, y_ref, pbuf, *, tp):
    s = pl.program_id(1)

    @pl.when(s == 0)
    def _():
        pbuf[0:POOL_HALO, :] = jnp.zeros((POOL_HALO, POOL_WIDTH), F32)

    pbuf[POOL_HALO:POOL_HALO + tp, :] = p_ref[0]
    pos1 = s * tp + lax.broadcasted_iota(jnp.int32, (tp, 1), 0) + 1
    for gi, win in enumerate(POOL_WINDOWS):
        cols = slice(gi * POOL_GROUP_DIM, (gi + 1) * POOL_GROUP_DIM)
        ext = pbuf[:, cols]
        acc = ext
        shift = 1
        while shift < win:
            acc = acc + pltpu.roll(acc, shift, 0)
            shift *= 2
        cnt = jnp.minimum(pos1, win).astype(F32)
        pooled = acc[POOL_HALO:, :] / cnt - ext[POOL_HALO:, :]
        mixed = _dot(pooled.astype(BF16), pw_ref[gi])
        y_ref[0, :, cols] = (mixed * ps_ref[:, cols]).astype(y_ref.dtype)
    pbuf[0:POOL_HALO, :] = pbuf[tp:tp + POOL_HALO, :]


def _pool(proj, pool_w, pool_scale, *, tp=256):
    b, s, n = proj.shape
    return pl.pallas_call(
        functools.partial(_pool_kernel, tp=tp),
        grid=(b, s // tp),
        in_specs=[
            pl.BlockSpec((1, tp, POOL_WIDTH), lambda i, j: (i, j, n // POOL_WIDTH - 1)),
            pl.BlockSpec((len(POOL_WINDOWS), POOL_GROUP_DIM, POOL_GROUP_DIM), lambda i, j: (0, 0, 0)),
            pl.BlockSpec((1, POOL_WIDTH), lambda i, j: (0, 0)),
        ],
        out_specs=pl.BlockSpec((1, tp, POOL_WIDTH), lambda i, j: (i, j, 0)),
        out_shape=jax.ShapeDtypeStruct((b, s, POOL_WIDTH), BF16),
        scratch_shapes=[pltpu.VMEM((tp + POOL_HALO, POOL_WIDTH), F32)],
        compiler_params=_cparams("parallel", "arbitrary"),
        name="pool",
    )(proj, pool_w, pool_scale)


def _outproj_kernel(h_ref, y1_ref, y2_ref, w1_ref, w2_ref, o_ref):
    o_ref[...] = h_ref[...] + (_dot(y1_ref[...], w1_ref[...]) + _dot(y2_ref[...], w2_ref[...]))


def _outproj(h, y1, y2, w1, w2, *, tm=512):
    t, d = h.shape
    k1, k2 = y1.shape[1], y2.shape[1]
    return pl.pallas_call(
        _outproj_kernel,
        grid=(t // tm,),
        in_specs=[
            pl.BlockSpec((tm, d), lambda i: (i, 0)),
            pl.BlockSpec((tm, k1), lambda i: (i, 0)),
            pl.BlockSpec((tm, k2), lambda i: (i, 0)),
            pl.BlockSpec((k1, d), lambda i: (0, 0)),
            pl.BlockSpec((k2, d), lambda i: (0, 0)),
        ],
        out_specs=pl.BlockSpec((tm, d), lambda i: (i, 0)),
        out_shape=jax.ShapeDtypeStruct((t, d), F32),
        compiler_params=_cparams("parallel"),
        name="outproj",
    )(h, y1, y2, w1, w2)


def _xattn_kernel(h_ref, g_ref, wq_ref, kv_ref, wo_ref, o_ref):
    h = h_ref[0]
    hn = _rmsnorm(h, g_ref[...]).astype(BF16)
    q = _dot(hn, wq_ref[...]).astype(BF16)
    outs = []
    for hd in range(X_HEADS):
        qh = q[:, hd * X_HEAD_DIM:(hd + 1) * X_HEAD_DIM]
        kh = kv_ref[0, :, hd * X_HEAD_DIM:(hd + 1) * X_HEAD_DIM]
        vh = kv_ref[0, :, D_MODEL + hd * X_HEAD_DIM:D_MODEL + (hd + 1) * X_HEAD_DIM]
        sc = _dot_nt(qh, kh) * (X_HEAD_DIM ** -0.5)
        sc = sc - jnp.max(sc, axis=-1, keepdims=True)
        e = jnp.exp(sc)
        p = e / jnp.sum(e, axis=-1, keepdims=True)
        outs.append(_dot(p.astype(BF16), vh).astype(BF16))
    o = jnp.concatenate(outs, axis=1)
    o_ref[0] = h + _dot(o, wo_ref[...])


def _xattn(h, g, wq, kv, wo, *, tm=256):
    b, s, d = h.shape
    return pl.pallas_call(
        _xattn_kernel,
        grid=(b, s // tm),
        in_specs=[
            pl.BlockSpec((1, tm, d), lambda i, j: (i, j, 0)),
            pl.BlockSpec((1, d), lambda i, j: (0, 0)),
            pl.BlockSpec((d, d), lambda i, j: (0, 0)),
            pl.BlockSpec((1, MEM_LEN, 2 * d), lambda i, j: (i, 0, 0)),
            pl.BlockSpec((d, d), lambda i, j: (0, 0)),
        ],
        out_specs=pl.BlockSpec((1, tm, d), lambda i, j: (i, j, 0)),
        out_shape=jax.ShapeDtypeStruct((b, s, d), F32),
        compiler_params=_cparams("parallel", "parallel"),
        name="xattn",
    )(h, g, wq, kv, wo)


def kernel(x, mem, ffn1_norm, ffn1_w_gate, ffn1_w_up, ffn1_w_down, mix_norm, w_in, conv_w, a_log, dt_bias, dn_out_norm, pool_w, pool_scale, w_out, xattn_norm, mem_norm, xattn_wq, xattn_wkv, xattn_wo, ffn2_norm, ffn2_w_gate, ffn2_w_up, ffn2_w_down, final_norm):
    bsz, seq, d = x.shape
    depth = w_in.shape[0]
    t = bsz * seq
    o_z = 3 * DN_WIDTH
    o_a = 4 * DN_WIDTH
    o_b = o_a + DN_HEADS
    o_p = o_b + DN_HEADS

    def row(v):
        return v.reshape(1, -1).astype(F32)

    def pad_lanes(v):
        return jnp.pad(v.astype(F32), (0, LANES - v.shape[0])).reshape(1, LANES)

    xs = x.reshape(t, d)
    mem2 = mem.reshape(bsz * MEM_LEN, d)
    fg = row(final_norm)
    for l in range(depth):
        w_main = jnp.concatenate([w_in[l][:, :o_a], w_in[l][:, o_p:]], axis=1).astype(BF16)
        w_ab = jnp.zeros((d, 2 * LANES), F32)
        w_ab = w_ab.at[:, 0:DN_HEADS].set(w_in[l][:, o_a:o_b])
        w_ab = w_ab.at[:, LANES:LANES + DN_HEADS].set(w_in[l][:, o_b:o_p]).astype(BF16)

        hs = _ffn(xs, row(ffn1_norm[l]), ffn1_w_gate[l].astype(BF16), ffn1_w_up[l].astype(BF16),
                  ffn1_w_down[l].astype(BF16), fg, final_norm=False)

        proj, ab = _inproj(hs, row(mix_norm[l]), w_main, w_ab)
        proj3 = proj.reshape(bsz, seq, -1)
        y_dn = _deltanet(proj3, ab.reshape(bsz, seq, -1), conv_w[l].astype(F32), pad_lanes(a_log[l]),
                         pad_lanes(dt_bias[l]), row(dn_out_norm[l]))
        y_pool = _pool(proj3, pool_w[l].astype(BF16), row(pool_scale[l]))
        w_o = w_out[l].astype(BF16)
        hs = _outproj(hs, y_dn.reshape(t, -1), y_pool.reshape(t, -1), w_o[:DN_WIDTH], w_o[DN_WIDTH:])

        kv = _norm_matmul(mem2, row(mem_norm[l]), xattn_wkv[l].astype(BF16), BF16, tm=512, tn=1024)
        hs = _xattn(hs.reshape(bsz, seq, d), row(xattn_norm[l]), xattn_wq[l].astype(BF16),
                    kv.reshape(bsz, MEM_LEN, 2 * d), xattn_wo[l].astype(BF16)).reshape(t, d)

        xs = _ffn(hs, row(ffn2_norm[l]), ffn2_w_gate[l].astype(BF16), ffn2_w_up[l].astype(BF16),
                  ffn2_w_down[l].astype(BF16), fg, final_norm=(l == depth - 1))
    return xs.reshape(bsz, seq, d)
```

```python
import functools

import jax
import jax.numpy as jnp
from jax import lax
from jax.experimental import pallas as pl
from jax.experimental.pallas import tpu as pltpu

F32 = jnp.float32
BF16 = jnp.bfloat16

D_MODEL = 2048
MEM_LEN = 256
DN_HEADS = 8
DN_HEAD_DIM = 128
DN_WIDTH = DN_HEADS * DN_HEAD_DIM
POOL_WINDOWS = (2, 4, 8, 16)
POOL_WIDTH = D_MODEL - DN_WIDTH
POOL_GROUP_DIM = POOL_WIDTH // len(POOL_WINDOWS)
POOL_HALO = 16
CONV_WIDTH = 4
CONV_HALO = 8
CHUNK = 64
HEAD_GROUP = 8
X_HEADS = 4
X_HEAD_DIM = D_MODEL // X_HEADS
EPS = 1e-6
LANES = 128
VMEM_LIMIT = 56 * 1024 * 1024


def _cparams(*sem):
    return pltpu.CompilerParams(dimension_semantics=sem, vmem_limit_bytes=VMEM_LIMIT)


def _rmsnorm(x, g):
    ms = jnp.mean(x * x, axis=-1, keepdims=True)
    return x * lax.rsqrt(ms + EPS) * g


def _sigmoid(x):
    return 1.0 / (1.0 + jnp.exp(-x))


def _dot(a, b):
    return jnp.dot(a, b, preferred_element_type=F32)


def _dot_nt(a, b):
    return lax.dot_general(a, b, (((1,), (1,)), ((), ())), preferred_element_type=F32)


def _dot_tn(a, b):
    return lax.dot_general(a, b, (((0,), (0,)), ((), ())), preferred_element_type=F32)


def _split2(x):
    hi = x.astype(BF16)
    lo = (x - hi.astype(F32)).astype(BF16)
    return hi, lo


def _split3(x):
    hi = x.astype(BF16)
    r1 = x - hi.astype(F32)
    mid = r1.astype(BF16)
    lo = (r1 - mid.astype(F32)).astype(BF16)
    return hi, mid, lo


def _dot_hp(a, b):
    ah, al = _split2(a)
    bh, bl = _split2(b)
    return _dot(ah, bh) + (_dot(ah, bl) + _dot(al, bh))


def _ffn_kernel(x_ref, g_ref, wg_ref, wu_ref, wd_ref, fg_ref, o_ref, xn_ref, acc_ref, *, final_norm):
    j = pl.program_id(1)

    @pl.when(j == 0)
    def _():
        xn_ref[...] = _rmsnorm(x_ref[...], g_ref[...]).astype(BF16)

    xn = xn_ref[...]
    gate = _dot(xn, wg_ref[...])
    up = _dot(xn, wu_ref[...])
    hid = (gate * _sigmoid(gate) * up).astype(BF16)
    part = _dot(hid, wd_ref[...])

    @pl.when(j == 0)
    def _():
        acc_ref[...] = part

    @pl.when(j > 0)
    def _():
        acc_ref[...] += part

    @pl.when(j == pl.num_programs(1) - 1)
    def _():
        y = x_ref[...] + 0.5 * acc_ref[...]
        if final_norm:
            y = _rmsnorm(y, fg_ref[...])
        o_ref[...] = y


def _ffn(x, g, wg, wu, wd, fg, *, final_norm, tm=512, tf=512):
    t, d = x.shape
    dff = wg.shape[1]
    return pl.pallas_call(
        functools.partial(_ffn_kernel, final_norm=final_norm),
        grid=(t // tm, dff // tf),
        in_specs=[
            pl.BlockSpec((tm, d), lambda i, j: (i, 0)),
            pl.BlockSpec((1, d), lambda i, j: (0, 0)),
            pl.BlockSpec((d, tf), lambda i, j: (0, j)),
            pl.BlockSpec((d, tf), lambda i, j: (0, j)),
            pl.BlockSpec((tf, d), lambda i, j: (j, 0)),
            pl.BlockSpec((1, d), lambda i, j: (0, 0)),
        ],
        out_specs=pl.BlockSpec((tm, d), lambda i, j: (i, 0)),
        out_shape=jax.ShapeDtypeStruct((t, d), F32),
        scratch_shapes=[pltpu.VMEM((tm, d), BF16), pltpu.VMEM((tm, d), F32)],
        compiler_params=_cparams("parallel", "arbitrary"),
        name="ffn",
    )(x, g, wg, wu, wd, fg)


def _nmm_kernel(x_ref, g_ref, w_ref, o_ref, xn_ref):
    @pl.when(pl.program_id(1) == 0)
    def _():
        xn_ref[...] = _rmsnorm(x_ref[...], g_ref[...]).astype(BF16)

    o_ref[...] = _dot(xn_ref[...], w_ref[...]).astype(o_ref.dtype)


def _norm_matmul(x, g, w, out_dtype, *, tm, tn):
    t, d = x.shape
    n = w.shape[1]
    return pl.pallas_call(
        _nmm_kernel,
        grid=(t // tm, n // tn),
        in_specs=[
            pl.BlockSpec((tm, d), lambda i, j: (i, 0)),
            pl.BlockSpec((1, d), lambda i, j: (0, 0)),
            pl.BlockSpec((d, tn), lambda i, j: (0, j)),
        ],
        out_specs=pl.BlockSpec((tm, tn), lambda i, j: (i, j)),
        out_shape=jax.ShapeDtypeStruct((t, n), out_dtype),
        scratch_shapes=[pltpu.VMEM((tm, d), BF16)],
        compiler_params=_cparams("parallel", "arbitrary"),
        name="norm_matmul",
    )(x, g, w)


def _inproj_kernel(x_ref, g_ref, w_ref, wab_ref, o_ref, ab_ref, xn_ref):
    @pl.when(pl.program_id(1) == 0)
    def _():
        xn = _rmsnorm(x_ref[...], g_ref[...]).astype(BF16)
        xn_ref[...] = xn
        ab_ref[...] = _dot(xn, wab_ref[...])

    o_ref[...] = _dot(xn_ref[...], w_ref[...])


def _inproj(x, g, w, wab, *, tm=1024, tn=512):
    t, d = x.shape
    n = w.shape[1]
    nab = wab.shape[1]
    return pl.pallas_call(
        _inproj_kernel,
        grid=(t // tm, n // tn),
        in_specs=[
            pl.BlockSpec((tm, d), lambda i, j: (i, 0)),
            pl.BlockSpec((1, d), lambda i, j: (0, 0)),
            pl.BlockSpec((d, tn), lambda i, j: (0, j)),
            pl.BlockSpec((d, nab), lambda i, j: (0, 0)),
        ],
        out_specs=[
            pl.BlockSpec((tm, tn), lambda i, j: (i, j)),
            pl.BlockSpec((tm, nab), lambda i, j: (i, 0)),
        ],
        out_shape=[jax.ShapeDtypeStruct((t, n), F32), jax.ShapeDtypeStruct((t, nab), F32)],
        scratch_shapes=[pltpu.VMEM((tm, d), BF16)],
        compiler_params=_cparams("parallel", "arbitrary"),
        name="inproj",
    )(x, g, w, wab)


def _dn_kernel(qkv_ref, z_ref, ab_ref, cw_ref, alog_ref, dtb_ref, on_ref, y_ref,
               xbuf, act, gc_s, gct_s, beta_s, state, u_s, w_s, qk_s, qd_s, kd_s, *, tc):
    s = pl.program_id(1)
    nchunk = tc // CHUNK
    qkv_w = 3 * DN_WIDTH

    @pl.when(s == 0)
    def _():
        xbuf[0:CONV_HALO, :] = jnp.zeros((CONV_HALO, qkv_w), F32)
        state[...] = jnp.zeros_like(state)

    xbuf[CONV_HALO:CONV_HALO + tc, :] = qkv_ref[0]

    def conv_block(cb, carry):
        col = pl.ds(pl.multiple_of(cb * LANES, LANES), LANES)
        w = cw_ref[:, col]
        acc = xbuf[CONV_HALO:CONV_HALO + tc, col] * w[3:4, :]
        for j in range(CONV_WIDTH - 1):
            off = CONV_HALO - (CONV_WIDTH - 1) + j
            acc = acc + xbuf[off:off + tc, col] * w[j:j + 1, :]
        y = acc * _sigmoid(acc)
        nrm = lax.rsqrt(jnp.sum(y * y, axis=-1, keepdims=True) + EPS)
        mult = jnp.where(cb < DN_HEADS, nrm * (DN_HEAD_DIM ** -0.5),
                         jnp.where(cb < 2 * DN_HEADS, nrm, 1.0))
        act[:, col] = y * mult
        return carry

    lax.fori_loop(0, qkv_w // LANES, conv_block, 0)
    xbuf[0:CONV_HALO, :] = xbuf[tc:tc + CONV_HALO, :]

    ab = ab_ref[0]
    xg = ab[:, 0:LANES] + dtb_ref[...]
    softplus = jnp.maximum(xg, 0.0) + jnp.log1p(jnp.exp(-jnp.abs(xg)))
    g = -jnp.exp(alog_ref[...]) * softplus
    beta_s[...] = _sigmoid(ab[:, LANES:2 * LANES])

    ri = lax.broadcasted_iota(jnp.int32, (tc, tc), 0)
    ci = lax.broadcasted_iota(jnp.int32, (tc, tc), 1)
    same_chunk = jnp.right_shift(ri, 6) == jnp.right_shift(ci, 6)
    tri = jnp.where(same_chunk, jnp.where(ci <= ri, 1.0, 0.0), 0.0).astype(BF16)
    gh, gm, gl = _split3(g)
    gc = _dot(tri, gh) + _dot(tri, gm) + _dot(tri, gl)
    gc_s[...] = gc

    er = lax.broadcasted_iota(jnp.int32, (LANES, LANES), 0)
    ec = lax.broadcasted_iota(jnp.int32, (LANES, LANES), 1)
    eye = jnp.where(er == ec, 1.0, 0.0).astype(BF16)
    for c in range(nchunk):
        ch, cm, cl = _split3(gc[c * CHUNK:(c + 1) * CHUNK, :])
        t = _dot_nt(eye, ch) + _dot_nt(eye, cm) + _dot_nt(eye, cl)
        gct_s[c] = t[0:DN_HEADS, :]

    r64 = lax.broadcasted_iota(jnp.int32, (CHUNK, CHUNK), 0)
    c64 = lax.broadcasted_iota(jnp.int32, (CHUNK, CHUNK), 1)
    causal = r64 >= c64
    strict = r64 > c64
    eye64 = jnp.where(r64 == c64, 1.0, 0.0)
    onorm = on_ref[...]

    def head_cols(h, base=0):
        return slice(base + h * DN_HEAD_DIM, base + (h + 1) * DN_HEAD_DIM)

    def hp3(a, b):
        (ah, al), (bh, bl) = a, b
        return _dot(ah, bh) + (_dot(ah, bl) + _dot(al, bh))

    def solve_chunk(c, carry):
        rows = pl.ds(pl.multiple_of(c * CHUNK, CHUNK), CHUNK)
        gcc = gc_s[rows, :]
        gct = gct_s[c]
        bet = beta_s[rows, :]
        egc = jnp.exp(gcc)
        ekd = jnp.exp(gcc[CHUNK - 1:CHUNK, :] - gcc)
        for h0 in range(0, DN_HEADS, HEAD_GROUP):
            hs = range(h0, h0 + HEAD_GROUP)
            q = [act[rows, head_cols(h)] for h in hs]
            k = [act[rows, head_cols(h, DN_WIDTH)] for h in hs]
            v = [act[rows, head_cols(h, 2 * DN_WIDTH)] for h in hs]
            bcol = [bet[:, h:h + 1] for h in hs]
            ecol = [egc[:, h:h + 1] for h in hs]
            n = range(HEAD_GROUP)
            kb = [k[i] * bcol[i] for i in n]
            k16 = [k[i].astype(BF16) for i in n]
            q16 = [q[i].astype(BF16) for i in n]
            dec = [jnp.where(causal, jnp.exp(gcc[:, h:h + 1] - gct[h:h + 1, :]), 0.0) for h in hs]
            kk = [_dot_nt(kb[i].astype(BF16), k16[i]) for i in n]
            qk = [_dot_nt(q16[i], k16[i]) for i in n]
            a_mat = [jnp.where(strict, kk[i] * dec[i], 0.0) for i in n]
            tinv = [eye64 - a_mat[i] for i in n]
            ap = [_split2(a_mat[i]) for i in n]
            for _ in range(5):
                sq = [hp3(ap[i], ap[i]) for i in n]
                ap = [_split2(sq[i]) for i in n]
                ts = [_split2(tinv[i]) for i in n]
                tinv = [tinv[i] + hp3(ts[i], ap[i]) for i in n]
            ts = [_split2(tinv[i]) for i in n]
            rhs = [_split2(jnp.concatenate([v[i] * bcol[i], kb[i] * ecol[i]], axis=1)) for i in n]
            uw = [hp3(ts[i], rhs[i]) for i in n]
            for i, h in enumerate(hs):
                u_s[rows, head_cols(h)] = uw[i][:, 0:DN_HEAD_DIM]
                w_s[rows, head_cols(h)] = uw[i][:, DN_HEAD_DIM:2 * DN_HEAD_DIM].astype(BF16)
                qk_s[h, rows, :] = (qk[i] * dec[i]).astype(BF16)
                qd_s[rows, head_cols(h)] = (q[i] * ecol[i]).astype(BF16)
                kd_s[rows, head_cols(h)] = (k[i] * ekd[:, h:h + 1]).astype(BF16)
        return carry

    lax.fori_loop(0, nchunk, solve_chunk, 0)

    def scan_chunk(c, carry):
        r0 = pl.multiple_of(c * CHUNK, CHUNK)
        rows = pl.ds(r0, CHUNK)
        egl = jnp.exp(gc_s[pl.ds(r0 + CHUNK - 1, 1), :])
        hs = range(DN_HEADS)
        st = [state[h] for h in hs]
        st16 = [st[h].astype(BF16) for h in hs]
        ws = [_dot(w_s[rows, head_cols(h)], st16[h]) for h in hs]
        vn16 = [(u_s[rows, head_cols(h)] - ws[h]).astype(BF16) for h in hs]
        o = [_dot(qd_s[rows, head_cols(h)], st16[h]) + _dot(qk_s[h, rows, :], vn16[h]) for h in hs]
        upd = [_dot_tn(kd_s[rows, head_cols(h)], vn16[h]) for h in hs]
        for h in hs:
            state[h] = st[h] * egl[:, h:h + 1] + upd[h]
            zh = z_ref[0, rows, head_cols(h)]
            y = _rmsnorm(o[h], onorm) * (zh * _sigmoid(zh))
            y_ref[0, rows, head_cols(h)] = y.astype(y_ref.dtype)
        return carry

    lax.fori_loop(0, nchunk, scan_chunk, 0)


def _deltanet(proj, ab, conv_w, alog, dtb, onorm, *, tc=256):
    b, s, _ = proj.shape
    qkv_w = 3 * DN_WIDTH
    return pl.pallas_call(
        functools.partial(_dn_kernel, tc=tc),
        grid=(b, s // tc),
        in_specs=[
            pl.BlockSpec((1, tc, qkv_w), lambda i, j: (i, j, 0)),
            pl.BlockSpec((1, tc, DN_WIDTH), lambda i, j: (i, j, qkv_w // DN_WIDTH)),
            pl.BlockSpec((1, tc, 2 * LANES), lambda i, j: (i, j, 0)),
            pl.BlockSpec((CONV_WIDTH, qkv_w), lambda i, j: (0, 0)),
            pl.BlockSpec((1, LANES), lambda i, j: (0, 0)),
            pl.BlockSpec((1, LANES), lambda i, j: (0, 0)),
            pl.BlockSpec((1, DN_HEAD_DIM), lambda i, j: (0, 0)),
        ],
        out_specs=pl.BlockSpec((1, tc, DN_WIDTH), lambda i, j: (i, j, 0)),
        out_shape=jax.ShapeDtypeStruct((b, s, DN_WIDTH), BF16),
        scratch_shapes=[
            pltpu.VMEM((tc + CONV_HALO, qkv_w), F32),
            pltpu.VMEM((tc, qkv_w), F32),
            pltpu.VMEM((tc, LANES), F32),
            pltpu.VMEM((tc // CHUNK, DN_HEADS, CHUNK), F32),
            pltpu.VMEM((tc, LANES), F32),
            pltpu.VMEM((DN_HEADS, DN_HEAD_DIM, DN_HEAD_DIM), F32),
            pltpu.VMEM((tc, DN_WIDTH), F32),
            pltpu.VMEM((tc, DN_WIDTH), BF16),
            pltpu.VMEM((DN_HEADS, tc, CHUNK), BF16),
            pltpu.VMEM((tc, DN_WIDTH), BF16),
            pltpu.VMEM((tc, DN_WIDTH), BF16),
        ],
        compiler_params=_cparams("parallel", "arbitrary"),
        name="deltanet",
    )(proj, proj, ab, conv_w, alog, dtb, onorm)


def _pool_kernel(p_ref, pw_ref, ps_ref, y_ref, pbuf, *, tp):
    s = pl.program_id(1)

    @pl.when(s == 0)
    def _():
        pbuf[0:POOL_HALO, :] = jnp.zeros((POOL_HALO, POOL_WIDTH), F32)

    pbuf[POOL_HALO:POOL_HALO + tp, :] = p_ref[0]
    pos1 = s * tp + lax.broadcasted_iota(jnp.int32, (tp, 1), 0) + 1
    for gi, win in enumerate(POOL_WINDOWS):
        cols = slice(gi * POOL_GROUP_DIM, (gi + 1) * POOL_GROUP_DIM)
        ext = pbuf[:, cols]
        acc = ext
        shift = 1
        while shift < win:
            acc = acc + pltpu.roll(acc, shift, 0)
            shift *= 2
        cnt = jnp.minimum(pos1, win).astype(F32)
        pooled = acc[POOL_HALO:, :] / cnt - ext[POOL_HALO:, :]
        mixed = _dot(pooled.astype(BF16), pw_ref[gi])
        y_ref[0, :, cols] = (mixed * ps_ref[:, cols]).astype(y_ref.dtype)
    pbuf[0:POOL_HALO, :] = pbuf[tp:tp + POOL_HALO, :]


def _pool(proj, pool_w, pool_scale, *, tp=256):
    b, s, n = proj.shape
    return pl.pallas_call(
        functools.partial(_pool_kernel, tp=tp),
        grid=(b, s // tp),
        in_specs=[
            pl.BlockSpec((1, tp, POOL_WIDTH), lambda i, j: (i, j, n // POOL_WIDTH - 1)),
            pl.BlockSpec((len(POOL_WINDOWS), POOL_GROUP_DIM, POOL_GROUP_DIM), lambda i, j: (0, 0, 0)),
            pl.BlockSpec((1, POOL_WIDTH), lambda i, j: (0, 0)),
        ],
        out_specs=pl.BlockSpec((1, tp, POOL_WIDTH), lambda i, j: (i, j, 0)),
        out_shape=jax.ShapeDtypeStruct((b, s, POOL_WIDTH), BF16),
        scratch_shapes=[pltpu.VMEM((tp + POOL_HALO, POOL_WIDTH), F32)],
        compiler_params=_cparams("parallel", "arbitrary"),
        name="pool",
    )(proj, pool_w, pool_scale)


def _outproj_kernel(h_ref, y1_ref, y2_ref, w1_ref, w2_ref, o_ref):
    o_ref[...] = h_ref[...] + (_dot(y1_ref[...], w1_ref[...]) + _dot(y2_ref[...], w2_ref[...]))


def _outproj(h, y1, y2, w1, w2, *, tm=512):
    t, d = h.shape
    k1, k2 = y1.shape[1], y2.shape[1]
    return pl.pallas_call(
        _outproj_kernel,
        grid=(t // tm,),
        in_specs=[
            pl.BlockSpec((tm, d), lambda i: (i, 0)),
            pl.BlockSpec((tm, k1), lambda i: (i, 0)),
            pl.BlockSpec((tm, k2), lambda i: (i, 0)),
            pl.BlockSpec((k1, d), lambda i: (0, 0)),
            pl.BlockSpec((k2, d), lambda i: (0, 0)),
        ],
        out_specs=pl.BlockSpec((tm, d), lambda i: (i, 0)),
        out_shape=jax.ShapeDtypeStruct((t, d), F32),
        compiler_params=_cparams("parallel"),
        name="outproj",
    )(h, y1, y2, w1, w2)


def _xattn_kernel(h_ref, g_ref, wq_ref, kv_ref, wo_ref, o_ref):
    h = h_ref[0]
    hn = _rmsnorm(h, g_ref[...]).astype(BF16)
    q = _dot(hn, wq_ref[...]).astype(BF16)
    outs = []
    for hd in range(X_HEADS):
        qh = q[:, hd * X_HEAD_DIM:(hd + 1) * X_HEAD_DIM]
        kh = kv_ref[0, :, hd * X_HEAD_DIM:(hd + 1) * X_HEAD_DIM]
        vh = kv_ref[0, :, D_MODEL + hd * X_HEAD_DIM:D_MODEL + (hd + 1) * X_HEAD_DIM]
        sc = _dot_nt(qh, kh) * (X_HEAD_DIM ** -0.5)
        sc = sc - jnp.max(sc, axis=-1, keepdims=True)
        e = jnp.exp(sc)
        p = e / jnp.sum(e, axis=-1, keepdims=True)
        outs.append(_dot(p.astype(BF16), vh).astype(BF16))
    o = jnp.concatenate(outs, axis=1)
    o_ref[0] = h + _dot(o, wo_ref[...])


def _xattn(h, g, wq, kv, wo, *, tm=256):
    b, s, d = h.shape
    return pl.pallas_call(
        _xattn_kernel,
        grid=(b, s // tm),
        in_specs=[
            pl.BlockSpec((1, tm, d), lambda i, j: (i, j, 0)),
            pl.BlockSpec((1, d), lambda i, j: (0, 0)),
            pl.BlockSpec((d, d), lambda i, j: (0, 0)),
            pl.BlockSpec((1, MEM_LEN, 2 * d), lambda i, j: (i, 0, 0)),
            pl.BlockSpec((d, d), lambda i, j: (0, 0)),
        ],
        out_specs=pl.BlockSpec((1, tm, d), lambda i, j: (i, j, 0)),
        out_shape=jax.ShapeDtypeStruct((b, s, d), F32),
        compiler_params=_cparams("parallel", "parallel"),
        name="xattn",
    )(h, g, wq, kv, wo)


def kernel(x, mem, ffn1_norm, ffn1_w_gate, ffn1_w_up, ffn1_w_down, mix_norm, w_in, conv_w, a_log, dt_bias, dn_out_norm, pool_w, pool_scale, w_out, xattn_norm, mem_norm, xattn_wq, xattn_wkv, xattn_wo, ffn2_norm, ffn2_w_gate, ffn2_w_up, ffn2_w_down, final_norm):
    bsz, seq, d = x.shape
    depth = w_in.shape[0]
    t = bsz * seq
    o_z = 3 * DN_WIDTH
    o_a = 4 * DN_WIDTH
    o_b = o_a + DN_HEADS
    o_p = o_b + DN_HEADS

    def row(v):
        return v.reshape(1, -1).astype(F32)

    def pad_lanes(v):
        return jnp.pad(v.astype(F32), (0, LANES - v.shape[0])).reshape(1, LANES)

    xs = x.reshape(t, d)
    mem2 = mem.reshape(bsz * MEM_LEN, d)
    fg = row(final_norm)
    for l in range(depth):
        w_main = jnp.concatenate([w_in[l][:, :o_a], w_in[l][:, o_p:]], axis=1).astype(BF16)
        w_ab = jnp.zeros((d, 2 * LANES), F32)
        w_ab = w_ab.at[:, 0:DN_HEADS].set(w_in[l][:, o_a:o_b])
        w_ab = w_ab.at[:, LANES:LANES + DN_HEADS].set(w_in[l][:, o_b:o_p]).astype(BF16)

        hs = _ffn(xs, row(ffn1_norm[l]), ffn1_w_gate[l].astype(BF16), ffn1_w_up[l].astype(BF16),
                  ffn1_w_down[l].astype(BF16), fg, final_norm=False)

        proj, ab = _inproj(hs, row(mix_norm[l]), w_main, w_ab)
        proj3 = proj.reshape(bsz, seq, -1)
        y_dn = _deltanet(proj3, ab.reshape(bsz, seq, -1), conv_w[l].astype(F32), pad_lanes(a_log[l]),
                         pad_lanes(dt_bias[l]), row(dn_out_norm[l]))
        y_pool = _pool(proj3, pool_w[l].astype(BF16), row(pool_scale[l]))
        w_o = w_out[l].astype(BF16)
        hs = _outproj(hs, y_dn.reshape(t, -1), y_pool.reshape(t, -1), w_o[:DN_WIDTH], w_o[DN_WIDTH:])

        kv = _norm_matmul(mem2, row(mem_norm[l]), xattn_wkv[l].astype(BF16), BF16, tm=512, tn=1024)
        hs = _xattn(hs.reshape(bsz, seq, d), row(xattn_norm[l]), xattn_wq[l].astype(BF16),
                    kv.reshape(bsz, MEM_LEN, 2 * d), xattn_wo[l].astype(BF16)).reshape(t, d)

        xs = _ffn(hs, row(ffn2_norm[l]), ffn2_w_gate[l].astype(BF16), ffn2_w_up[l].astype(BF16),
                  ffn2_w_down[l].astype(BF16), fg, final_norm=(l == depth - 1))
    return xs.reshape(bsz, seq, d)
```

```python
import functools

import jax
import jax.numpy as jnp
from jax import lax
from jax.experimental import pallas as pl
from jax.experimental.pallas import tpu as pltpu

F32 = jnp.float32
BF16 = jnp.bfloat16

D_MODEL = 2048
MEM_LEN = 256
DN_HEADS = 8
DN_HEAD_DIM = 128
DN_WIDTH = DN_HEADS * DN_HEAD_DIM
POOL_WINDOWS = (2, 4, 8, 16)
POOL_WIDTH = D_MODEL - DN_WIDTH
POOL_GROUP_DIM = POOL_WIDTH // len(POOL_WINDOWS)
POOL_HALO = 16
CONV_WIDTH = 4
CONV_HALO = 8
CHUNK = 64
X_HEADS = 4
X_HEAD_DIM = D_MODEL // X_HEADS
EPS = 1e-6
LANES = 128
VMEM_LIMIT = 60 * 1024 * 1024


def _cparams(*sem):
    return pltpu.CompilerParams(dimension_semantics=sem, vmem_limit_bytes=VMEM_LIMIT)


def _rmsnorm(x, g):
    ms = jnp.mean(x * x, axis=-1, keepdims=True)
    return x * lax.rsqrt(ms + EPS) * g


def _sigmoid(x):
    return 1.0 / (1.0 + jnp.exp(-x))


def _dot(a, b):
    return jnp.dot(a, b, preferred_element_type=F32)


def _dot_nt(a, b):
    return lax.dot_general(a, b, (((1,), (1,)), ((), ())), preferred_element_type=F32)


def _dot_tn(a, b):
    return lax.dot_general(a, b, (((0,), (0,)), ((), ())), preferred_element_type=F32)


def _split2(x):
    hi = x.astype(BF16)
    lo = (x - hi.astype(F32)).astype(BF16)
    return hi, lo


def _split3(x):
    hi = x.astype(BF16)
    r1 = x - hi.astype(F32)
    mid = r1.astype(BF16)
    lo = (r1 - mid.astype(F32)).astype(BF16)
    return hi, mid, lo


def _dot_hp(a, b):
    (ah, al), (bh, bl) = a, b
    return _dot(ah, bh) + (_dot(ah, bl) + _dot(al, bh))


def _cast_kernel(x_ref, o_ref):
    o_ref[...] = x_ref[...].astype(o_ref.dtype)


def _to_bf16(w, *, tr=512):
    shape = w.shape
    w2 = w.reshape(-1, shape[-1])
    r, c = w2.shape
    out = pl.pallas_call(
        _cast_kernel,
        grid=(r // tr,),
        in_specs=[pl.BlockSpec((tr, c), lambda i: (i, 0))],
        out_specs=pl.BlockSpec((tr, c), lambda i: (i, 0)),
        out_shape=jax.ShapeDtypeStruct((r, c), BF16),
        compiler_params=_cparams("parallel"),
        name="to_bf16",
    )(w2)
    return out.reshape(shape)


def _ffn_kernel(x_ref, g_ref, wg_ref, wu_ref, wd_ref, fg_ref, o_ref, xn_ref, *, final_norm):
    j = pl.program_id(1)

    @pl.when(j == 0)
    def _():
        xn_ref[...] = _rmsnorm(x_ref[...], g_ref[...]).astype(BF16)
        o_ref[...] = jnp.zeros_like(o_ref)

    xn = xn_ref[...]
    gate = _dot(xn, wg_ref[...])
    up = _dot(xn, wu_ref[...])
    hid = (gate * _sigmoid(gate) * up).astype(BF16)
    o_ref[...] += _dot(hid, wd_ref[...])

    @pl.when(j == pl.num_programs(1) - 1)
    def _():
        y = x_ref[...] + 0.5 * o_ref[...]
        if final_norm:
            y = _rmsnorm(y, fg_ref[...])
        o_ref[...] = y


def _ffn(x, g, wg, wu, wd, fg, layer, *, final_norm, tm=1024, tf=512):
    t, d = x.shape
    dff = wg.shape[2]
    return pl.pallas_call(
        functools.partial(_ffn_kernel, final_norm=final_norm),
        grid=(t // tm, dff // tf),
        in_specs=[
            pl.BlockSpec((tm, d), lambda i, j: (i, 0)),
            pl.BlockSpec((1, d), lambda i, j: (0, 0)),
            pl.BlockSpec((None, d, tf), lambda i, j: (layer, 0, j)),
            pl.BlockSpec((None, d, tf), lambda i, j: (layer, 0, j)),
            pl.BlockSpec((None, tf, d), lambda i, j: (layer, j, 0)),
            pl.BlockSpec((1, d), lambda i, j: (0, 0)),
        ],
        out_specs=pl.BlockSpec((tm, d), lambda i, j: (i, 0)),
        out_shape=jax.ShapeDtypeStruct((t, d), F32),
        scratch_shapes=[pltpu.VMEM((tm, d), BF16)],
        compiler_params=_cparams("parallel", "arbitrary"),
        name="ffn",
    )(x, g, wg, wu, wd, fg)


def _nmm_kernel(x_ref, g_ref, w_ref, o_ref, xn_ref):
    @pl.when(pl.program_id(1) == 0)
    def _():
        xn_ref[...] = _rmsnorm(x_ref[...], g_ref[...]).astype(BF16)

    o_ref[...] = _dot(xn_ref[...], w_ref[...]).astype(o_ref.dtype)


def _norm_matmul(x, g, w, layer, out_dtype, *, tm, tn):
    t, d = x.shape
    n = w.shape[2]
    return pl.pallas_call(
        _nmm_kernel,
        grid=(t // tm, n // tn),
        in_specs=[
            pl.BlockSpec((tm, d), lambda i, j: (i, 0)),
            pl.BlockSpec((1, d), lambda i, j: (0, 0)),
            pl.BlockSpec((None, d, tn), lambda i, j: (layer, 0, j)),
        ],
        out_specs=pl.BlockSpec((tm, tn), lambda i, j: (i, j)),
        out_shape=jax.ShapeDtypeStruct((t, n), out_dtype),
        scratch_shapes=[pltpu.VMEM((tm, d), BF16)],
        compiler_params=_cparams("parallel", "arbitrary"),
        name="norm_matmul",
    )(x, g, w)


def _inproj_kernel(x_ref, g_ref, w_ref, wab_ref, o_ref, ab_ref, xn_ref):
    @pl.when(pl.program_id(1) == 0)
    def _():
        xn = _rmsnorm(x_ref[...], g_ref[...]).astype(BF16)
        xn_ref[...] = xn
        ab_ref[...] = _dot(xn, wab_ref[...])

    o_ref[...] = _dot(xn_ref[...], w_ref[...])


def _inproj(x, g, w, wab, layer, *, tm=1024, tn=1024):
    t, d = x.shape
    n = w.shape[2]
    nab = wab.shape[2]
    return pl.pallas_call(
        _inproj_kernel,
        grid=(t // tm, n // tn),
        in_specs=[
            pl.BlockSpec((tm, d), lambda i, j: (i, 0)),
            pl.BlockSpec((1, d), lambda i, j: (0, 0)),
            pl.BlockSpec((None, d, tn), lambda i, j: (layer, 0, j)),
            pl.BlockSpec((None, d, nab), lambda i, j: (layer, 0, 0)),
        ],
        out_specs=[
            pl.BlockSpec((tm, tn), lambda i, j: (i, j)),
            pl.BlockSpec((tm, nab), lambda i, j: (i, 0)),
        ],
        out_shape=[jax.ShapeDtypeStruct((t, n), F32), jax.ShapeDtypeStruct((t, nab), F32)],
        scratch_shapes=[pltpu.VMEM((tm, d), BF16)],
        compiler_params=_cparams("parallel", "arbitrary"),
        name="inproj",
    )(x, g, w, wab)


def _dn_kernel(qkv_ref, z_ref, ab_ref, cw_ref, alog_ref, dtb_ref, on_ref, y_ref, xbuf, state, *, tc):
    s = pl.program_id(1)
    nchunk = tc // CHUNK
    qkv_w = 3 * DN_WIDTH
    heads = range(DN_HEADS)

    @pl.when(s == 0)
    def _():
        xbuf[0:CONV_HALO, :] = jnp.zeros((CONV_HALO, qkv_w), F32)
        state[...] = jnp.zeros_like(state)

    xbuf[CONV_HALO:CONV_HALO + tc, :] = qkv_ref[0]

    ab = ab_ref[0]
    xg = ab[:, 0:LANES] + dtb_ref[...]
    softplus = jnp.maximum(xg, 0.0) + jnp.log1p(jnp.exp(-jnp.abs(xg)))
    g = -jnp.exp(alog_ref[...]) * softplus
    beta = _sigmoid(ab[:, LANES:2 * LANES])

    ri = lax.broadcasted_iota(jnp.int32, (tc, tc), 0)
    ci = lax.broadcasted_iota(jnp.int32, (tc, tc), 1)
    same_chunk = jnp.right_shift(ri, 6) == jnp.right_shift(ci, 6)
    tri = jnp.where(same_chunk, jnp.where(ci <= ri, 1.0, 0.0), 0.0).astype(BF16)
    gh, gm, gl = _split3(g)
    gc = _dot(tri, gh) + _dot(tri, gm) + _dot(tri, gl)

    er = lax.broadcasted_iota(jnp.int32, (LANES, LANES), 0)
    ec = lax.broadcasted_iota(jnp.int32, (LANES, LANES), 1)
    eye = jnp.where(er == ec, 1.0, 0.0).astype(BF16)
    gct = []
    for c in range(nchunk):
        ch, cm, cl = _split3(gc[c * CHUNK:(c + 1) * CHUNK, :])
        t = _dot_nt(eye, ch) + _dot_nt(eye, cm) + _dot_nt(eye, cl)
        gct.append(t[0:DN_HEADS, :])

    r64 = lax.broadcasted_iota(jnp.int32, (CHUNK, CHUNK), 0)
    c64 = lax.broadcasted_iota(jnp.int32, (CHUNK, CHUNK), 1)
    causal = r64 >= c64
    strict = r64 > c64
    eye64 = jnp.where(r64 == c64, 1.0, 0.0)
    onorm = on_ref[...]

    def head_cols(h, base=0):
        return slice(base + h * DN_HEAD_DIM, base + (h + 1) * DN_HEAD_DIM)

    def conv_act(c, cols, norm_scale):
        r0 = c * CHUNK + CONV_HALO
        w = cw_ref[:, cols]
        acc = xbuf[r0:r0 + CHUNK, cols] * w[CONV_WIDTH - 1:CONV_WIDTH, :]
        for j in range(CONV_WIDTH - 1):
            off = r0 - (CONV_WIDTH - 1) + j
            acc = acc + xbuf[off:off + CHUNK, cols] * w[j:j + 1, :]
        y = acc * _sigmoid(acc)
        if norm_scale is None:
            return y
        return y * (lax.rsqrt(jnp.sum(y * y, axis=-1, keepdims=True) + EPS) * norm_scale)

    def solve_chunk(c):
        rows = slice(c * CHUNK, (c + 1) * CHUNK)
        gcc = gc[rows, :]
        bet = beta[rows, :]
        egc = jnp.exp(gcc)
        ekd = jnp.exp(gcc[CHUNK - 1:CHUNK, :] - gcc)
        q = [conv_act(c, head_cols(h), DN_HEAD_DIM ** -0.5) for h in heads]
        k = [conv_act(c, head_cols(h, DN_WIDTH), 1.0) for h in heads]
        v = [conv_act(c, head_cols(h, 2 * DN_WIDTH), None) for h in heads]
        bcol = [bet[:, h:h + 1] for h in heads]
        ecol = [egc[:, h:h + 1] for h in heads]
        kb = [k[h] * bcol[h] for h in heads]
        k16 = [k[h].astype(BF16) for h in heads]
        q16 = [q[h].astype(BF16) for h in heads]
        dec = [jnp.where(causal, jnp.exp(gcc[:, h:h + 1] - gct[c][h:h + 1, :]), 0.0) for h in heads]
        kk = [_dot_nt(kb[h].astype(BF16), k16[h]) for h in heads]
        qk = [_dot_nt(q16[h], k16[h]) for h in heads]
        a_mat = [jnp.where(strict, kk[h] * dec[h], 0.0) for h in heads]
        tinv = [eye64 - a_mat[h] for h in heads]
        ap = [_split2(a_mat[h]) for h in heads]
        for _ in range(5):
            sq = [_dot_hp(ap[h], ap[h]) for h in heads]
            ap = [_split2(sq[h]) for h in heads]
            ts = [_split2(tinv[h]) for h in heads]
            tinv = [tinv[h] + _dot_hp(ts[h], ap[h]) for h in heads]
        ts = [_split2(tinv[h]) for h in heads]
        rhs = [_split2(jnp.concatenate([v[h] * bcol[h], kb[h] * ecol[h]], axis=1)) for h in heads]
        uw = [_dot_hp(ts[h], rhs[h]) for h in heads]
        return dict(
            u=[uw[h][:, 0:DN_HEAD_DIM] for h in heads],
            w=[uw[h][:, DN_HEAD_DIM:2 * DN_HEAD_DIM].astype(BF16) for h in heads],
            qk=[(qk[h] * dec[h]).astype(BF16) for h in heads],
            qd=[(q[h] * ecol[h]).astype(BF16) for h in heads],
            kd=[(k[h] * ekd[:, h:h + 1]).astype(BF16) for h in heads],
        )

    def scan_chunk(c, sol):
        rows = slice(c * CHUNK, (c + 1) * CHUNK)
        egl = jnp.exp(gc[(c + 1) * CHUNK - 1:(c + 1) * CHUNK, :])
        st = [state[h] for h in heads]
        st16 = [st[h].astype(BF16) for h in heads]
        ws = [_dot(sol["w"][h], st16[h]) for h in heads]
        vn16 = [(sol["u"][h] - ws[h]).astype(BF16) for h in heads]
        o = [_dot(sol["qd"][h], st16[h]) + _dot(sol["qk"][h], vn16[h]) for h in heads]
        upd = [_dot_tn(sol["kd"][h], vn16[h]) for h in heads]
        for h in heads:
            state[h] = st[h] * egl[:, h:h + 1] + upd[h]
            zh = z_ref[0, rows, head_cols(h)]
            y = _rmsnorm(o[h], onorm) * (zh * _sigmoid(zh))
            y_ref[0, rows, head_cols(h)] = y.astype(y_ref.dtype)

    sol = solve_chunk(0)
    for c in range(1, nchunk):
        nxt = solve_chunk(c)
        scan_chunk(c - 1, sol)
        sol = nxt
    scan_chunk(nchunk - 1, sol)

    xbuf[0:CONV_HALO, :] = xbuf[tc:tc + CONV_HALO, :]


def _deltanet(proj, ab, conv_w, alog, dtb, onorm, *, tc=256):
    b, s, _ = proj.shape
    qkv_w = 3 * DN_WIDTH
    return pl.pallas_call(
        functools.partial(_dn_kernel, tc=tc),
        grid=(b, s // tc),
        in_specs=[
            pl.BlockSpec((1, tc, qkv_w), lambda i, j: (i, j, 0)),
            pl.BlockSpec((1, tc, DN_WIDTH), lambda i, j: (i, j, qkv_w // DN_WIDTH)),
            pl.BlockSpec((1, tc, 2 * LANES), lambda i, j: (i, j, 0)),
            pl.BlockSpec((CONV_WIDTH, qkv_w), lambda i, j: (0, 0)),
            pl.BlockSpec((1, LANES), lambda i, j: (0, 0)),
            pl.BlockSpec((1, LANES), lambda i, j: (0, 0)),
            pl.BlockSpec((1, DN_HEAD_DIM), lambda i, j: (0, 0)),
        ],
        out_specs=pl.BlockSpec((1, tc, DN_WIDTH), lambda i, j: (i, j, 0)),
        out_shape=jax.ShapeDtypeStruct((b, s, DN_WIDTH), BF16),
        scratch_shapes=[
            pltpu.VMEM((tc + CONV_HALO, qkv_w), F32),
            pltpu.VMEM((DN_HEADS, DN_HEAD_DIM, DN_HEAD_DIM), F32),
        ],
        compiler_params=_cparams("parallel", "arbitrary"),
        name="deltanet",
    )(proj, proj, ab, conv_w, alog, dtb, onorm)


def _pool_kernel(p_ref, pw_ref, ps_ref, y_ref, pbuf, *, tp):
    s = pl.program_id(1)

    @pl.when(s == 0)
    def _():
        pbuf[0:POOL_HALO, :] = jnp.zeros((POOL_HALO, POOL_WIDTH), F32)

    pbuf[POOL_HALO:POOL_HALO + tp, :] = p_ref[0]
    pos1 = s * tp + lax.broadcasted_iota(jnp.int32, (tp, 1), 0) + 1
    for gi, win in enumerate(POOL_WINDOWS):
        cols = slice(gi * POOL_GROUP_DIM, (gi + 1) * POOL_GROUP_DIM)
        ext = pbuf[:, cols]
        acc = ext
        shift = 1
        while shift < win:
            acc = acc + pltpu.roll(acc, shift, 0)
            shift *= 2
        cnt = jnp.minimum(pos1, win).astype(F32)
        pooled = acc[POOL_HALO:, :] / cnt - ext[POOL_HALO:, :]
        mixed = _dot(pooled.astype(BF16), pw_ref[gi])
        y_ref[0, :, cols] = (mixed * ps_ref[:, cols]).astype(y_ref.dtype)
    pbuf[0:POOL_HALO, :] = pbuf[tp:tp + POOL_HALO, :]


def _pool(proj, pool_w, pool_scale, layer, *, tp=256):
    b, s, n = proj.shape
    ngroups = len(POOL_WINDOWS)
    return pl.pallas_call(
        functools.partial(_pool_kernel, tp=tp),
        grid=(b, s // tp),
        in_specs=[
            pl.BlockSpec((1, tp, POOL_WIDTH), lambda i, j: (i, j, n // POOL_WIDTH - 1)),
            pl.BlockSpec((None, ngroups, POOL_GROUP_DIM, POOL_GROUP_DIM), lambda i, j: (layer, 0, 0, 0)),
            pl.BlockSpec((1, POOL_WIDTH), lambda i, j: (0, 0)),
        ],
        out_specs=pl.BlockSpec((1, tp, POOL_WIDTH), lambda i, j: (i, j, 0)),
        out_shape=jax.ShapeDtypeStruct((b, s, POOL_WIDTH), BF16),
        scratch_shapes=[pltpu.VMEM((tp + POOL_HALO, POOL_WIDTH), F32)],
        compiler_params=_cparams("parallel", "arbitrary"),
        name="pool",
    )(proj, pool_w, pool_scale)


def _outproj_kernel(h_ref, y1_ref, y2_ref, w1_ref, w2_ref, o_ref):
    o_ref[...] = h_ref[...] + (_dot(y1_ref[...], w1_ref[...]) + _dot(y2_ref[...], w2_ref[...]))


def _outproj(h, y1, y2, w, layer, *, tm=512):
    t, d = h.shape
    k1, k2 = y1.shape[1], y2.shape[1]
    assert k1 == k2 and k1 + k2 == w.shape[1]
    return pl.pallas_call(
        _outproj_kernel,
        grid=(t // tm,),
        in_specs=[
            pl.BlockSpec((tm, d), lambda i: (i, 0)),
            pl.BlockSpec((tm, k1), lambda i: (i, 0)),
            pl.BlockSpec((tm, k2), lambda i: (i, 0)),
            pl.BlockSpec((None, k1, d), lambda i: (layer, 0, 0)),
            pl.BlockSpec((None, k2, d), lambda i: (layer, 1, 0)),
        ],
        out_specs=pl.BlockSpec((tm, d), lambda i: (i, 0)),
        out_shape=jax.ShapeDtypeStruct((t, d), F32),
        compiler_params=_cparams("parallel"),
        name="outproj",
    )(h, y1, y2, w, w)


def _xattn_kernel(h_ref, g_ref, wq_ref, kv_ref, wo_ref, o_ref):
    h = h_ref[0]
    hn = _rmsnorm(h, g_ref[...]).astype(BF16)
    q = _dot(hn, wq_ref[...]).astype(BF16)
    outs = []
    for hd in range(X_HEADS):
        qh = q[:, hd * X_HEAD_DIM:(hd + 1) * X_HEAD_DIM]
        kh = kv_ref[0, :, hd * X_HEAD_DIM:(hd + 1) * X_HEAD_DIM]
        vh = kv_ref[0, :, D_MODEL + hd * X_HEAD_DIM:D_MODEL + (hd + 1) * X_HEAD_DIM]
        sc = _dot_nt(qh, kh) * (X_HEAD_DIM ** -0.5)
        sc = sc - jnp.max(sc, axis=-1, keepdims=True)
        e = jnp.exp(sc)
        p = e / jnp.sum(e, axis=-1, keepdims=True)
        outs.append(_dot(p.astype(BF16), vh).astype(BF16))
    o = jnp.concatenate(outs, axis=1)
    o_ref[0] = h + _dot(o, wo_ref[...])


def _xattn(h, g, wq, kv, wo, layer, *, tm=256):
    b, s, d = h.shape
    return pl.pallas_call(
        _xattn_kernel,
        grid=(b, s // tm),
        in_specs=[
            pl.BlockSpec((1, tm, d), lambda i, j: (i, j, 0)),
            pl.BlockSpec((1, d), lambda i, j: (0, 0)),
            pl.BlockSpec((None, d, d), lambda i, j: (layer, 0, 0)),
            pl.BlockSpec((1, MEM_LEN, 2 * d), lambda i, j: (i, 0, 0)),
            pl.BlockSpec((None, d, d), lambda i, j: (layer, 0, 0)),
        ],
        out_specs=pl.BlockSpec((1, tm, d), lambda i, j: (i, j, 0)),
        out_shape=jax.ShapeDtypeStruct((b, s, d), F32),
        compiler_params=_cparams("parallel", "parallel"),
        name="xattn",
    )(h, g, wq, kv, wo)


def kernel(x, mem, ffn1_norm, ffn1_w_gate, ffn1_w_up, ffn1_w_down, mix_norm, w_in, conv_w, a_log, dt_bias, dn_out_norm, pool_w, pool_scale, w_out, xattn_norm, mem_norm, xattn_wq, xattn_wkv, xattn_wo, ffn2_norm, ffn2_w_gate, ffn2_w_up, ffn2_w_down, final_norm):
    bsz, seq, d = x.shape
    depth = w_in.shape[0]
    t = bsz * seq
    o_a = 4 * DN_WIDTH
    o_b = o_a + DN_HEADS
    o_p = o_b + DN_HEADS

    def row(v):
        return v.reshape(1, -1).astype(F32)

    def pad_lanes(v):
        return jnp.pad(v.astype(F32), (0, LANES - v.shape[0])).reshape(1, LANES)

    f1g, f1u, f1d = _to_bf16(ffn1_w_gate), _to_bf16(ffn1_w_up), _to_bf16(ffn1_w_down)
    f2g, f2u, f2d = _to_bf16(ffn2_w_gate), _to_bf16(ffn2_w_up), _to_bf16(ffn2_w_down)
    wq16, wkv16, wo16 = _to_bf16(xattn_wq), _to_bf16(xattn_wkv), _to_bf16(xattn_wo)
    wout16, poolw16 = _to_bf16(w_out), _to_bf16(pool_w)
    win16 = _to_bf16(w_in)
    w_main = jnp.concatenate([win16[:, :, :o_a], win16[:, :, o_p:]], axis=2)
    w_ab = jnp.zeros((depth, d, 2 * LANES), BF16)
    w_ab = w_ab.at[:, :, 0:DN_HEADS].set(win16[:, :, o_a:o_b])
    w_ab = w_ab.at[:, :, LANES:LANES + DN_HEADS].set(win16[:, :, o_b:o_p])

    xs = x.reshape(t, d)
    mem2 = mem.reshape(bsz * MEM_LEN, d)
    fg = row(final_norm)
    for l in range(depth):
        hs = _ffn(xs, row(ffn1_norm[l]), f1g, f1u, f1d, fg, l, final_norm=False)

        proj, ab = _inproj(hs, row(mix_norm[l]), w_main, w_ab, l)
        proj3 = proj.reshape(bsz, seq, -1)
        y_dn = _deltanet(proj3, ab.reshape(bsz, seq, -1), conv_w[l].astype(F32), pad_lanes(a_log[l]),
                         pad_lanes(dt_bias[l]), row(dn_out_norm[l]))
        y_pool = _pool(proj3, poolw16, row(pool_scale[l]), l)
        hs = _outproj(hs, y_dn.reshape(t, -1), y_pool.reshape(t, -1), wout16, l)

        kv = _norm_matmul(mem2, row(mem_norm[l]), wkv16, l, BF16, tm=512, tn=1024)
        hs = _xattn(hs.reshape(bsz, seq, d), row(xattn_norm[l]), wq16, kv.reshape(bsz, MEM_LEN, 2 * d),
                    wo16, l).reshape(t, d)

        last = l == depth - 1
        xs = _ffn(hs, row(ffn2_norm[l]), f2g, f2u, f2d, fg, l, final_norm=last, tm=512 if last else 1024)
    return xs.reshape(bsz, seq, d)
```

```python
import functools

import jax
import jax.numpy as jnp
from jax import lax
from jax.experimental import pallas as pl
from jax.experimental.pallas import tpu as pltpu

F32 = jnp.float32
BF16 = jnp.bfloat16

D_MODEL = 2048
MEM_LEN = 256
DN_HEADS = 8
DN_HEAD_DIM = 128
DN_WIDTH = DN_HEADS * DN_HEAD_DIM
POOL_WINDOWS = (2, 4, 8, 16)
POOL_WIDTH = D_MODEL - DN_WIDTH
POOL_GROUP_DIM = POOL_WIDTH // len(POOL_WINDOWS)
POOL_HALO = 16
CONV_WIDTH = 4
CONV_HALO = 8
CHUNK = 64
X_HEADS = 4
X_HEAD_DIM = D_MODEL // X_HEADS
EPS = 1e-6
LANES = 128
ROW_SLAB = 256
VMEM_LIMIT = 60 * 1024 * 1024


def _cparams(*sem):
    return pltpu.CompilerParams(dimension_semantics=sem, vmem_limit_bytes=VMEM_LIMIT)


def _rmsnorm(x, g):
    ms = jnp.mean(x * x, axis=-1, keepdims=True)
    return x * lax.rsqrt(ms + EPS) * g


def _sigmoid(x):
    return 1.0 / (1.0 + jnp.exp(-x))


def _dot(a, b):
    return jnp.dot(a, b, preferred_element_type=F32)


def _dot_nt(a, b):
    return lax.dot_general(a, b, (((1,), (1,)), ((), ())), preferred_element_type=F32)


def _dot_tn(a, b):
    return lax.dot_general(a, b, (((0,), (0,)), ((), ())), preferred_element_type=F32)


def _split2(x):
    hi = x.astype(BF16)
    lo = (x - hi.astype(F32)).astype(BF16)
    return hi, lo


def _split3(x):
    hi = x.astype(BF16)
    r1 = x - hi.astype(F32)
    mid = r1.astype(BF16)
    lo = (r1 - mid.astype(F32)).astype(BF16)
    return hi, mid, lo


def _dot_hp(a, b):
    (ah, al), (bh, bl) = a, b
    return _dot(ah, bh) + (_dot(ah, bl) + _dot(al, bh))


def _cast_kernel(x_ref, o_ref):
    o_ref[...] = x_ref[...].astype(o_ref.dtype)


def _to_bf16(w, *, tr=512):
    shape = w.shape
    w2 = w.reshape(-1, shape[-1])
    r, c = w2.shape
    out = pl.pallas_call(
        _cast_kernel,
        grid=(r // tr,),
        in_specs=[pl.BlockSpec((tr, c), lambda i: (i, 0))],
        out_specs=pl.BlockSpec((tr, c), lambda i: (i, 0)),
        out_shape=jax.ShapeDtypeStruct((r, c), BF16),
        compiler_params=_cparams("parallel"),
        name="to_bf16",
    )(w2)
    return out.reshape(shape)


def _ffn_kernel(x_ref, g_ref, wg_ref, wu_ref, wd_ref, fg_ref, o_ref, xn_ref, *, final_norm):
    j = pl.program_id(1)

    @pl.when(j == 0)
    def _():
        for r in range(0, x_ref.shape[0], ROW_SLAB):
            rows = slice(r, r + ROW_SLAB)
            xn_ref[rows, :] = _rmsnorm(x_ref[rows, :], g_ref[...]).astype(BF16)
        o_ref[...] = jnp.zeros_like(o_ref)

    xn = xn_ref[...]
    gate = _dot(xn, wg_ref[...])
    up = _dot(xn, wu_ref[...])
    hid = (gate * _sigmoid(gate) * up).astype(BF16)
    o_ref[...] += _dot(hid, wd_ref[...])

    @pl.when(j == pl.num_programs(1) - 1)
    def _():
        for r in range(0, x_ref.shape[0], ROW_SLAB):
            rows = slice(r, r + ROW_SLAB)
            y = x_ref[rows, :] + 0.5 * o_ref[rows, :]
            if final_norm:
                y = _rmsnorm(y, fg_ref[...])
            o_ref[rows, :] = y


def _ffn(x, g, wg, wu, wd, fg, layer, *, final_norm, tm=1024, tf=512):
    t, d = x.shape
    dff = wg.shape[2]
    return pl.pallas_call(
        functools.partial(_ffn_kernel, final_norm=final_norm),
        grid=(t // tm, dff // tf),
        in_specs=[
            pl.BlockSpec((tm, d), lambda i, j: (i, 0)),
            pl.BlockSpec((1, d), lambda i, j: (0, 0)),
            pl.BlockSpec((None, d, tf), lambda i, j: (layer, 0, j)),
            pl.BlockSpec((None, d, tf), lambda i, j: (layer, 0, j)),
            pl.BlockSpec((None, tf, d), lambda i, j: (layer, j, 0)),
            pl.BlockSpec((1, d), lambda i, j: (0, 0)),
        ],
        out_specs=pl.BlockSpec((tm, d), lambda i, j: (i, 0)),
        out_shape=jax.ShapeDtypeStruct((t, d), F32),
        scratch_shapes=[pltpu.VMEM((tm, d), BF16)],
        compiler_params=_cparams("parallel", "arbitrary"),
        name="ffn",
    )(x, g, wg, wu, wd, fg)


def _nmm_kernel(x_ref, g_ref, w_ref, o_ref, xn_ref):
    @pl.when(pl.program_id(1) == 0)
    def _():
        xn_ref[...] = _rmsnorm(x_ref[...], g_ref[...]).astype(BF16)

    o_ref[...] = _dot(xn_ref[...], w_ref[...]).astype(o_ref.dtype)


def _norm_matmul(x, g, w, layer, out_dtype, *, tm, tn):
    t, d = x.shape
    n = w.shape[2]
    return pl.pallas_call(
        _nmm_kernel,
        grid=(t // tm, n // tn),
        in_specs=[
            pl.BlockSpec((tm, d), lambda i, j: (i, 0)),
            pl.BlockSpec((1, d), lambda i, j: (0, 0)),
            pl.BlockSpec((None, d, tn), lambda i, j: (layer, 0, j)),
        ],
        out_specs=pl.BlockSpec((tm, tn), lambda i, j: (i, j)),
        out_shape=jax.ShapeDtypeStruct((t, n), out_dtype),
        scratch_shapes=[pltpu.VMEM((tm, d), BF16)],
        compiler_params=_cparams("parallel", "arbitrary"),
        name="norm_matmul",
    )(x, g, w)


def _inproj_kernel(x_ref, g_ref, w_ref, wab_ref, o_ref, ab_ref, xn_ref):
    @pl.when(pl.program_id(1) == 0)
    def _():
        xn = _rmsnorm(x_ref[...], g_ref[...]).astype(BF16)
        xn_ref[...] = xn
        ab_ref[...] = _dot(xn, wab_ref[...])

    o_ref[...] = _dot(xn_ref[...], w_ref[...])


def _inproj(x, g, w, wab, layer, *, tm=1024, tn=1024):
    t, d = x.shape
    n = w.shape[2]
    nab = wab.shape[2]
    return pl.pallas_call(
        _inproj_kernel,
        grid=(t // tm, n // tn),
        in_specs=[
            pl.BlockSpec((tm, d), lambda i, j: (i, 0)),
            pl.BlockSpec((1, d), lambda i, j: (0, 0)),
            pl.BlockSpec((None, d, tn), lambda i, j: (layer, 0, j)),
            pl.BlockSpec((None, d, nab), lambda i, j: (layer, 0, 0)),
        ],
        out_specs=[
            pl.BlockSpec((tm, tn), lambda i, j: (i, j)),
            pl.BlockSpec((tm, nab), lambda i, j: (i, 0)),
        ],
        out_shape=[jax.ShapeDtypeStruct((t, n), F32), jax.ShapeDtypeStruct((t, nab), F32)],
        scratch_shapes=[pltpu.VMEM((tm, d), BF16)],
        compiler_params=_cparams("parallel", "arbitrary"),
        name="inproj",
    )(x, g, w, wab)


def _dn_kernel(qkv_ref, z_ref, ab_ref, cw_ref, alog_ref, dtb_ref, on_ref, y_ref, xbuf, state, *, tc):
    s = pl.program_id(1)
    nchunk = tc // CHUNK
    qkv_w = 3 * DN_WIDTH
    heads = range(DN_HEADS)

    @pl.when(s == 0)
    def _():
        xbuf[0:CONV_HALO, :] = jnp.zeros((CONV_HALO, qkv_w), F32)
        state[...] = jnp.zeros_like(state)

    xbuf[CONV_HALO:CONV_HALO + tc, :] = qkv_ref[0]

    ab = ab_ref[0]
    xg = ab[:, 0:LANES] + dtb_ref[...]
    softplus = jnp.maximum(xg, 0.0) + jnp.log1p(jnp.exp(-jnp.abs(xg)))
    g = -jnp.exp(alog_ref[...]) * softplus
    beta = _sigmoid(ab[:, LANES:2 * LANES])

    ri = lax.broadcasted_iota(jnp.int32, (tc, tc), 0)
    ci = lax.broadcasted_iota(jnp.int32, (tc, tc), 1)
    same_chunk = jnp.right_shift(ri, 6) == jnp.right_shift(ci, 6)
    tri = jnp.where(same_chunk, jnp.where(ci <= ri, 1.0, 0.0), 0.0).astype(BF16)
    gh, gm, gl = _split3(g)
    gc = _dot(tri, gh) + _dot(tri, gm) + _dot(tri, gl)

    er = lax.broadcasted_iota(jnp.int32, (LANES, LANES), 0)
    ec = lax.broadcasted_iota(jnp.int32, (LANES, LANES), 1)
    eye = jnp.where(er == ec, 1.0, 0.0).astype(BF16)
    gct = []
    for c in range(nchunk):
        ch, cm, cl = _split3(gc[c * CHUNK:(c + 1) * CHUNK, :])
        t = _dot_nt(eye, ch) + _dot_nt(eye, cm) + _dot_nt(eye, cl)
        gct.append(t[0:DN_HEADS, :])

    r64 = lax.broadcasted_iota(jnp.int32, (CHUNK, CHUNK), 0)
    c64 = lax.broadcasted_iota(jnp.int32, (CHUNK, CHUNK), 1)
    causal = r64 >= c64
    strict = r64 > c64
    eye64 = jnp.where(r64 == c64, 1.0, 0.0)
    blk = {n: jnp.right_shift(r64, n.bit_length() - 1) == jnp.right_shift(c64, n.bit_length() - 1)
           for n in (8, 16, 32, 64)}
    onorm = on_ref[...]

    def head_cols(h, base=0):
        return slice(base + h * DN_HEAD_DIM, base + (h + 1) * DN_HEAD_DIM)

    def conv_act(c, cols, norm_scale):
        r0 = c * CHUNK + CONV_HALO
        w = cw_ref[:, cols]
        acc = xbuf[r0:r0 + CHUNK, cols] * w[CONV_WIDTH - 1:CONV_WIDTH, :]
        for j in range(CONV_WIDTH - 1):
            off = r0 - (CONV_WIDTH - 1) + j
            acc = acc + xbuf[off:off + CHUNK, cols] * w[j:j + 1, :]
        y = acc * _sigmoid(acc)
        if norm_scale is None:
            return y
        return y * (lax.rsqrt(jnp.sum(y * y, axis=-1, keepdims=True) + EPS) * norm_scale)

    def solve_chunk(c):
        rows = slice(c * CHUNK, (c + 1) * CHUNK)
        gcc = gc[rows, :]
        bet = beta[rows, :]
        egc = jnp.exp(gcc)
        ekd = jnp.exp(gcc[CHUNK - 1:CHUNK, :] - gcc)
        q = [conv_act(c, head_cols(h), DN_HEAD_DIM ** -0.5) for h in heads]
        k = [conv_act(c, head_cols(h, DN_WIDTH), 1.0) for h in heads]
        v = [conv_act(c, head_cols(h, 2 * DN_WIDTH), None) for h in heads]
        bcol = [bet[:, h:h + 1] for h in heads]
        ecol = [egc[:, h:h + 1] for h in heads]
        kb = [k[h] * bcol[h] for h in heads]
        k16 = [k[h].astype(BF16) for h in heads]
        q16 = [q[h].astype(BF16) for h in heads]
        dec = [jnp.where(causal, jnp.exp(gcc[:, h:h + 1] - gct[c][h:h + 1, :]), 0.0) for h in heads]
        kk = [_dot_nt(kb[h].astype(BF16), k16[h]) for h in heads]
        qk = [_dot_nt(q16[h], k16[h]) for h in heads]
        a_mat = [jnp.where(strict, kk[h] * dec[h], 0.0) for h in heads]
        ad = [_split2(jnp.where(blk[8], a_mat[h], 0.0)) for h in heads]
        a2 = [_dot_hp(ad[h], ad[h]) for h in heads]
        a2s = [_split2(a2[h]) for h in heads]
        tinv = [eye64 - jnp.where(blk[8], a_mat[h], 0.0) for h in heads]
        tinv = [tinv[h] + _dot_hp(_split2(tinv[h]), a2s[h]) for h in heads]
        a4s = [_split2(_dot_hp(a2s[h], a2s[h])) for h in heads]
        tinv = [tinv[h] + _dot_hp(_split2(tinv[h]), a4s[h]) for h in heads]
        for n in (8, 16, 32):
            ts = [_split2(tinv[h]) for h in heads]
            low = [_split2(jnp.where(blk[2 * n], jnp.where(blk[n], 0.0, a_mat[h]), 0.0)) for h in heads]
            lx = [_split2(_dot_hp(low[h], ts[h])) for h in heads]
            tinv = [tinv[h] - _dot_hp(ts[h], lx[h]) for h in heads]
        ts = [_split2(tinv[h]) for h in heads]
        rhs = [_split2(jnp.concatenate([v[h] * bcol[h], kb[h] * ecol[h]], axis=1)) for h in heads]
        uw = [_dot_hp(ts[h], rhs[h]) for h in heads]
        return dict(
            u=[uw[h][:, 0:DN_HEAD_DIM] for h in heads],
            w=[uw[h][:, DN_HEAD_DIM:2 * DN_HEAD_DIM].astype(BF16) for h in heads],
            qk=[(qk[h] * dec[h]).astype(BF16) for h in heads],
            qd=[(q[h] * ecol[h]).astype(BF16) for h in heads],
            kd=[(k[h] * ekd[:, h:h + 1]).astype(BF16) for h in heads],
        )

    def scan_chunk(c, sol):
        rows = slice(c * CHUNK, (c + 1) * CHUNK)
        egl = jnp.exp(gc[(c + 1) * CHUNK - 1:(c + 1) * CHUNK, :])
        st = [state[h] for h in heads]
        st16 = [st[h].astype(BF16) for h in heads]
        ws = [_dot(sol["w"][h], st16[h]) for h in heads]
        vn16 = [(sol["u"][h] - ws[h]).astype(BF16) for h in heads]
        o = [_dot(sol["qd"][h], st16[h]) + _dot(sol["qk"][h], vn16[h]) for h in heads]
        upd = [_dot_tn(sol["kd"][h], vn16[h]) for h in heads]
        for h in heads:
            state[h] = st[h] * egl[:, h:h + 1] + upd[h]
            zh = z_ref[0, rows, head_cols(h)]
            y = _rmsnorm(o[h], onorm) * (zh * _sigmoid(zh))
            y_ref[0, rows, head_cols(h)] = y.astype(y_ref.dtype)

    sol = solve_chunk(0)
    for c in range(1, nchunk):
        nxt = solve_chunk(c)
        scan_chunk(c - 1, sol)
        sol = nxt
    scan_chunk(nchunk - 1, sol)

    xbuf[0:CONV_HALO, :] = xbuf[tc:tc + CONV_HALO, :]


def _deltanet(proj, ab, conv_w, alog, dtb, onorm, *, tc=256):
    b, s, _ = proj.shape
    qkv_w = 3 * DN_WIDTH
    return pl.pallas_call(
        functools.partial(_dn_kernel, tc=tc),
        grid=(b, s // tc),
        in_specs=[
            pl.BlockSpec((1, tc, qkv_w), lambda i, j: (i, j, 0)),
            pl.BlockSpec((1, tc, DN_WIDTH), lambda i, j: (i, j, qkv_w // DN_WIDTH)),
            pl.BlockSpec((1, tc, 2 * LANES), lambda i, j: (i, j, 0)),
            pl.BlockSpec((CONV_WIDTH, qkv_w), lambda i, j: (0, 0)),
            pl.BlockSpec((1, LANES), lambda i, j: (0, 0)),
            pl.BlockSpec((1, LANES), lambda i, j: (0, 0)),
            pl.BlockSpec((1, DN_HEAD_DIM), lambda i, j: (0, 0)),
        ],
        out_specs=pl.BlockSpec((1, tc, DN_WIDTH), lambda i, j: (i, j, 0)),
        out_shape=jax.ShapeDtypeStruct((b, s, DN_WIDTH), BF16),
        scratch_shapes=[
            pltpu.VMEM((tc + CONV_HALO, qkv_w), F32),
            pltpu.VMEM((DN_HEADS, DN_HEAD_DIM, DN_HEAD_DIM), F32),
        ],
        compiler_params=_cparams("parallel", "arbitrary"),
        name="deltanet",
    )(proj, proj, ab, conv_w, alog, dtb, onorm)


def _pool_kernel(p_ref, pw_ref, ps_ref, y_ref, pbuf, *, tp):
    s = pl.program_id(1)

    @pl.when(s == 0)
    def _():
        pbuf[0:POOL_HALO, :] = jnp.zeros((POOL_HALO, POOL_WIDTH), F32)

    pbuf[POOL_HALO:POOL_HALO + tp, :] = p_ref[0]
    pos1 = s * tp + lax.broadcasted_iota(jnp.int32, (tp, 1), 0) + 1
    for gi, win in enumerate(POOL_WINDOWS):
        cols = slice(gi * POOL_GROUP_DIM, (gi + 1) * POOL_GROUP_DIM)
        ext = pbuf[:, cols]
        acc = ext
        shift = 1
        while shift < win:
            acc = acc + pltpu.roll(acc, shift, 0)
            shift *= 2
        cnt = jnp.minimum(pos1, win).astype(F32)
        pooled = acc[POOL_HALO:, :] / cnt - ext[POOL_HALO:, :]
        mixed = _dot(pooled.astype(BF16), pw_ref[gi])
        y_ref[0, :, cols] = (mixed * ps_ref[:, cols]).astype(y_ref.dtype)
    pbuf[0:POOL_HALO, :] = pbuf[tp:tp + POOL_HALO, :]


def _pool(proj, pool_w, pool_scale, layer, *, tp=256):
    b, s, n = proj.shape
    ngroups = len(POOL_WINDOWS)
    return pl.pallas_call(
        functools.partial(_pool_kernel, tp=tp),
        grid=(b, s // tp),
        in_specs=[
            pl.BlockSpec((1, tp, POOL_WIDTH), lambda i, j: (i, j, n // POOL_WIDTH - 1)),
            pl.BlockSpec((None, ngroups, POOL_GROUP_DIM, POOL_GROUP_DIM), lambda i, j: (layer, 0, 0, 0)),
            pl.BlockSpec((1, POOL_WIDTH), lambda i, j: (0, 0)),
        ],
        out_specs=pl.BlockSpec((1, tp, POOL_WIDTH), lambda i, j: (i, j, 0)),
        out_shape=jax.ShapeDtypeStruct((b, s, POOL_WIDTH), BF16),
        scratch_shapes=[pltpu.VMEM((tp + POOL_HALO, POOL_WIDTH), F32)],
        compiler_params=_cparams("parallel", "arbitrary"),
        name="pool",
    )(proj, pool_w, pool_scale)


def _outproj_kernel(h_ref, y1_ref, y2_ref, w1_ref, w2_ref, o_ref):
    o_ref[...] = h_ref[...] + (_dot(y1_ref[...], w1_ref[...]) + _dot(y2_ref[...], w2_ref[...]))


def _outproj(h, y1, y2, w, layer, *, tm=512):
    t, d = h.shape
    k1, k2 = y1.shape[1], y2.shape[1]
    assert k1 == k2 and k1 + k2 == w.shape[1]
    return pl.pallas_call(
        _outproj_kernel,
        grid=(t // tm,),
        in_specs=[
            pl.BlockSpec((tm, d), lambda i: (i, 0)),
            pl.BlockSpec((tm, k1), lambda i: (i, 0)),
            pl.BlockSpec((tm, k2), lambda i: (i, 0)),
            pl.BlockSpec((None, k1, d), lambda i: (layer, 0, 0)),
            pl.BlockSpec((None, k2, d), lambda i: (layer, 1, 0)),
        ],
        out_specs=pl.BlockSpec((tm, d), lambda i: (i, 0)),
        out_shape=jax.ShapeDtypeStruct((t, d), F32),
        compiler_params=_cparams("parallel"),
        name="outproj",
    )(h, y1, y2, w, w)


def _xattn_kernel(h_ref, g_ref, wq_ref, kv_ref, wo_ref, o_ref):
    h = h_ref[0]
    hn = _rmsnorm(h, g_ref[...]).astype(BF16)
    q = _dot(hn, wq_ref[...]).astype(BF16)
    outs = []
    for hd in range(X_HEADS):
        qh = q[:, hd * X_HEAD_DIM:(hd + 1) * X_HEAD_DIM]
        kh = kv_ref[0, :, hd * X_HEAD_DIM:(hd + 1) * X_HEAD_DIM]
        vh = kv_ref[0, :, D_MODEL + hd * X_HEAD_DIM:D_MODEL + (hd + 1) * X_HEAD_DIM]
        sc = _dot_nt(qh, kh) * (X_HEAD_DIM ** -0.5)
        sc = sc - jnp.max(sc, axis=-1, keepdims=True)
        e = jnp.exp(sc)
        p = e / jnp.sum(e, axis=-1, keepdims=True)
        outs.append(_dot(p.astype(BF16), vh).astype(BF16))
    o = jnp.concatenate(outs, axis=1)
    o_ref[0] = h + _dot(o, wo_ref[...])


def _xattn(h, g, wq, kv, wo, layer, *, tm=512):
    b, s, d = h.shape
    return pl.pallas_call(
        _xattn_kernel,
        grid=(b, s // tm),
        in_specs=[
            pl.BlockSpec((1, tm, d), lambda i, j: (i, j, 0)),
            pl.BlockSpec((1, d), lambda i, j: (0, 0)),
            pl.BlockSpec((None, d, d), lambda i, j: (layer, 0, 0), pipeline_mode=pl.Buffered(1)),
            pl.BlockSpec((1, MEM_LEN, 2 * d), lambda i, j: (i, 0, 0)),
            pl.BlockSpec((None, d, d), lambda i, j: (layer, 0, 0), pipeline_mode=pl.Buffered(1)),
        ],
        out_specs=pl.BlockSpec((1, tm, d), lambda i, j: (i, j, 0)),
        out_shape=jax.ShapeDtypeStruct((b, s, d), F32),
        compiler_params=_cparams("parallel", "parallel"),
        name="xattn",
    )(h, g, wq, kv, wo)


def kernel(x, mem, ffn1_norm, ffn1_w_gate, ffn1_w_up, ffn1_w_down, mix_norm, w_in, conv_w, a_log, dt_bias, dn_out_norm, pool_w, pool_scale, w_out, xattn_norm, mem_norm, xattn_wq, xattn_wkv, xattn_wo, ffn2_norm, ffn2_w_gate, ffn2_w_up, ffn2_w_down, final_norm):
    bsz, seq, d = x.shape
    depth = w_in.shape[0]
    t = bsz * seq
    o_a = 4 * DN_WIDTH
    o_b = o_a + DN_HEADS
    o_p = o_b + DN_HEADS

    def row(v):
        return v.reshape(1, -1).astype(F32)

    def pad_lanes(v):
        return jnp.pad(v.astype(F32), (0, LANES - v.shape[0])).reshape(1, LANES)

    f1g, f1u, f1d = _to_bf16(ffn1_w_gate), _to_bf16(ffn1_w_up), _to_bf16(ffn1_w_down)
    f2g, f2u, f2d = _to_bf16(ffn2_w_gate), _to_bf16(ffn2_w_up), _to_bf16(ffn2_w_down)
    wq16, wkv16, wo16 = _to_bf16(xattn_wq), _to_bf16(xattn_wkv), _to_bf16(xattn_wo)
    wout16, poolw16 = _to_bf16(w_out), _to_bf16(pool_w)
    win16 = _to_bf16(w_in)
    w_main = jnp.concatenate([win16[:, :, :o_a], win16[:, :, o_p:]], axis=2)
    w_ab = jnp.zeros((depth, d, 2 * LANES), BF16)
    w_ab = w_ab.at[:, :, 0:DN_HEADS].set(win16[:, :, o_a:o_b])
    w_ab = w_ab.at[:, :, LANES:LANES + DN_HEADS].set(win16[:, :, o_b:o_p])

    xs = x.reshape(t, d)
    mem2 = mem.reshape(bsz * MEM_LEN, d)
    fg = row(final_norm)
    for l in range(depth):
        hs = _ffn(xs, row(ffn1_norm[l]), f1g, f1u, f1d, fg, l, final_norm=False)

        proj, ab = _inproj(hs, row(mix_norm[l]), w_main, w_ab, l)
        proj3 = proj.reshape(bsz, seq, -1)
        y_dn = _deltanet(proj3, ab.reshape(bsz, seq, -1), conv_w[l].astype(F32), pad_lanes(a_log[l]),
                         pad_lanes(dt_bias[l]), row(dn_out_norm[l]))
        y_pool = _pool(proj3, poolw16, row(pool_scale[l]), l)
        hs = _outproj(hs, y_dn.reshape(t, -1), y_pool.reshape(t, -1), wout16, l)

        kv = _norm_matmul(mem2, row(mem_norm[l]), wkv16, l, BF16, tm=512, tn=1024)
        hs = _xattn(hs.reshape(bsz, seq, d), row(xattn_norm[l]), wq16, kv.reshape(bsz, MEM_LEN, 2 * d),
                    wo16, l).reshape(t, d)

        last = l == depth - 1
        xs = _ffn(hs, row(ffn2_norm[l]), f2g, f2u, f2d, fg, l, final_norm=last, tm=512 if last else 1024)
    return xs.reshape(bsz, seq, d)
```

```python
import functools

import jax
import jax.numpy as jnp
from jax import lax
from jax.experimental import pallas as pl
from jax.experimental.pallas import tpu as pltpu

F32 = jnp.float32
BF16 = jnp.bfloat16

D_MODEL = 2048
MEM_LEN = 256
DN_HEADS = 8
DN_HEAD_DIM = 128
DN_WIDTH = DN_HEADS * DN_HEAD_DIM
POOL_WINDOWS = (2, 4, 8, 16)
POOL_WIDTH = D_MODEL - DN_WIDTH
POOL_GROUP_DIM = POOL_WIDTH // len(POOL_WINDOWS)
POOL_HALO = 16
CONV_WIDTH = 4
CONV_HALO = 8
CHUNK = 64
SOLVE_GROUP = 2
X_HEADS = 4
X_HEAD_DIM = D_MODEL // X_HEADS
EPS = 1e-6
LANES = 128
ROW_SLAB = 256
VMEM_LIMIT = 60 * 1024 * 1024


def _cparams(*sem):
    return pltpu.CompilerParams(dimension_semantics=sem, vmem_limit_bytes=VMEM_LIMIT)


def _rmsnorm(x, g):
    ms = jnp.mean(x * x, axis=-1, keepdims=True)
    return x * lax.rsqrt(ms + EPS) * g


def _sigmoid(x):
    return 1.0 / (1.0 + jnp.exp(-x))


def _dot(a, b):
    return jnp.dot(a, b, preferred_element_type=F32)


def _dot_nt(a, b):
    return lax.dot_general(a, b, (((1,), (1,)), ((), ())), preferred_element_type=F32)


def _dot_tn(a, b):
    return lax.dot_general(a, b, (((0,), (0,)), ((), ())), preferred_element_type=F32)


def _split2(x):
    hi = x.astype(BF16)
    lo = (x - hi.astype(F32)).astype(BF16)
    return hi, lo


def _split3(x):
    hi = x.astype(BF16)
    r1 = x - hi.astype(F32)
    mid = r1.astype(BF16)
    lo = (r1 - mid.astype(F32)).astype(BF16)
    return hi, mid, lo


def _dot_hp(a, b):
    (ah, al), (bh, bl) = a, b
    return _dot(ah, bh) + (_dot(ah, bl) + _dot(al, bh))


def _cast_kernel(x_ref, o_ref):
    o_ref[...] = x_ref[...].astype(o_ref.dtype)


def _to_bf16(w, *, tr=512):
    shape = w.shape
    w2 = w.reshape(-1, shape[-1])
    r, c = w2.shape
    out = pl.pallas_call(
        _cast_kernel,
        grid=(r // tr,),
        in_specs=[pl.BlockSpec((tr, c), lambda i: (i, 0))],
        out_specs=pl.BlockSpec((tr, c), lambda i: (i, 0)),
        out_shape=jax.ShapeDtypeStruct((r, c), BF16),
        compiler_params=_cparams("parallel"),
        name="to_bf16",
    )(w2)
    return out.reshape(shape)


def _ffn_kernel(x_ref, g_ref, wg_ref, wu_ref, wd_ref, fg_ref, o_ref, xn_ref, *, final_norm):
    j = pl.program_id(1)

    @pl.when(j == 0)
    def _():
        for r in range(0, x_ref.shape[0], ROW_SLAB):
            rows = slice(r, r + ROW_SLAB)
            xn_ref[rows, :] = _rmsnorm(x_ref[rows, :], g_ref[...]).astype(BF16)
        o_ref[...] = jnp.zeros_like(o_ref)

    xn = xn_ref[...]
    gate = _dot(xn, wg_ref[...])
    up = _dot(xn, wu_ref[...])
    hid = (gate * _sigmoid(gate) * up).astype(BF16)
    o_ref[...] += _dot(hid, wd_ref[...])

    @pl.when(j == pl.num_programs(1) - 1)
    def _():
        for r in range(0, x_ref.shape[0], ROW_SLAB):
            rows = slice(r, r + ROW_SLAB)
            y = x_ref[rows, :] + 0.5 * o_ref[rows, :]
            if final_norm:
                y = _rmsnorm(y, fg_ref[...])
            o_ref[rows, :] = y


def _ffn(x, g, wg, wu, wd, fg, layer, *, final_norm, tm=1024, tf=512):
    t, d = x.shape
    dff = wg.shape[2]
    return pl.pallas_call(
        functools.partial(_ffn_kernel, final_norm=final_norm),
        grid=(t // tm, dff // tf),
        in_specs=[
            pl.BlockSpec((tm, d), lambda i, j: (i, 0)),
            pl.BlockSpec((1, d), lambda i, j: (0, 0)),
            pl.BlockSpec((None, d, tf), lambda i, j: (layer, 0, j)),
            pl.BlockSpec((None, d, tf), lambda i, j: (layer, 0, j)),
            pl.BlockSpec((None, tf, d), lambda i, j: (layer, j, 0)),
            pl.BlockSpec((1, d), lambda i, j: (0, 0)),
        ],
        out_specs=pl.BlockSpec((tm, d), lambda i, j: (i, 0)),
        out_shape=jax.ShapeDtypeStruct((t, d), F32),
        scratch_shapes=[pltpu.VMEM((tm, d), BF16)],
        compiler_params=_cparams("parallel", "arbitrary"),
        name="ffn",
    )(x, g, wg, wu, wd, fg)


def _nmm_kernel(x_ref, g_ref, w_ref, o_ref, xn_ref):
    @pl.when(pl.program_id(1) == 0)
    def _():
        xn_ref[...] = _rmsnorm(x_ref[...], g_ref[...]).astype(BF16)

    o_ref[...] = _dot(xn_ref[...], w_ref[...]).astype(o_ref.dtype)


def _norm_matmul(x, g, w, layer, out_dtype, *, tm, tn):
    t, d = x.shape
    n = w.shape[2]
    return pl.pallas_call(
        _nmm_kernel,
        grid=(t // tm, n // tn),
        in_specs=[
            pl.BlockSpec((tm, d), lambda i, j: (i, 0)),
            pl.BlockSpec((1, d), lambda i, j: (0, 0)),
            pl.BlockSpec((None, d, tn), lambda i, j: (layer, 0, j)),
        ],
        out_specs=pl.BlockSpec((tm, tn), lambda i, j: (i, j)),
        out_shape=jax.ShapeDtypeStruct((t, n), out_dtype),
        scratch_shapes=[pltpu.VMEM((tm, d), BF16)],
        compiler_params=_cparams("parallel", "arbitrary"),
        name="norm_matmul",
    )(x, g, w)


def _inproj_kernel(x_ref, g_ref, wa_ref, wp_ref, wab_ref, o_ref, ab_ref, xn_ref, *, n_main):
    j = pl.program_id(1)

    @pl.when(j == 0)
    def _():
        xn = _rmsnorm(x_ref[...], g_ref[...]).astype(BF16)
        xn_ref[...] = xn
        ab_ref[...] = _dot(xn, wab_ref[...])

    @pl.when(j < n_main)
    def _():
        o_ref[...] = _dot(xn_ref[...], wa_ref[...])

    @pl.when(j >= n_main)
    def _():
        o_ref[...] = _dot(xn_ref[...], wp_ref[...])


def _inproj(x, g, w_in, w_pool, wab, layer, *, n_main_cols, tm=1024, tn=1024):
    t, d = x.shape
    assert w_pool.shape[2] == tn and n_main_cols % tn == 0
    n_main = n_main_cols // tn
    n = n_main_cols + tn
    nab = wab.shape[2]
    return pl.pallas_call(
        functools.partial(_inproj_kernel, n_main=n_main),
        grid=(t // tm, n // tn),
        in_specs=[
            pl.BlockSpec((tm, d), lambda i, j: (i, 0)),
            pl.BlockSpec((1, d), lambda i, j: (0, 0)),
            pl.BlockSpec((None, d, tn), lambda i, j: (layer, 0, jnp.minimum(j, n_main - 1))),
            pl.BlockSpec((None, d, tn), lambda i, j: (layer, 0, 0)),
            pl.BlockSpec((None, d, nab), lambda i, j: (layer, 0, 0)),
        ],
        out_specs=[
            pl.BlockSpec((tm, tn), lambda i, j: (i, j)),
            pl.BlockSpec((tm, nab), lambda i, j: (i, 0)),
        ],
        out_shape=[jax.ShapeDtypeStruct((t, n), F32), jax.ShapeDtypeStruct((t, nab), F32)],
        scratch_shapes=[pltpu.VMEM((tm, d), BF16)],
        compiler_params=_cparams("parallel", "arbitrary"),
        name="inproj",
    )(x, g, w_in, w_pool, wab)


def _dn_kernel(qkv_ref, z_ref, ab_ref, cw_ref, alog_ref, dtb_ref, on_ref, y_ref, xbuf, state, *, tc):
    s = pl.program_id(1)
    nchunk = tc // CHUNK
    qkv_w = 3 * DN_WIDTH
    heads = range(DN_HEADS)

    @pl.when(s == 0)
    def _():
        xbuf[0:CONV_HALO, :] = jnp.zeros((CONV_HALO, qkv_w), F32)
        state[...] = jnp.zeros_like(state)

    xbuf[CONV_HALO:CONV_HALO + tc, :] = qkv_ref[0]

    ab = ab_ref[0]
    xg = ab[:, 0:LANES] + dtb_ref[...]
    softplus = jnp.maximum(xg, 0.0) + jnp.log1p(jnp.exp(-jnp.abs(xg)))
    g = -jnp.exp(alog_ref[...]) * softplus
    beta = _sigmoid(ab[:, LANES:2 * LANES])

    ri = lax.broadcasted_iota(jnp.int32, (tc, tc), 0)
    ci = lax.broadcasted_iota(jnp.int32, (tc, tc), 1)
    same_chunk = jnp.right_shift(ri, 6) == jnp.right_shift(ci, 6)
    tri = jnp.where(same_chunk, jnp.where(ci <= ri, 1.0, 0.0), 0.0).astype(BF16)
    gh, gm, gl = _split3(g)
    gc = _dot(tri, gh) + _dot(tri, gm) + _dot(tri, gl)

    er = lax.broadcasted_iota(jnp.int32, (LANES, LANES), 0)
    ec = lax.broadcasted_iota(jnp.int32, (LANES, LANES), 1)
    eye = jnp.where(er == ec, 1.0, 0.0).astype(BF16)
    gct = []
    for c in range(nchunk):
        ch, cm, cl = _split3(gc[c * CHUNK:(c + 1) * CHUNK, :])
        t = _dot_nt(eye, ch) + _dot_nt(eye, cm) + _dot_nt(eye, cl)
        gct.append(t[0:DN_HEADS, :])

    r64 = lax.broadcasted_iota(jnp.int32, (CHUNK, CHUNK), 0)
    c64 = lax.broadcasted_iota(jnp.int32, (CHUNK, CHUNK), 1)
    causal = r64 >= c64
    strict = r64 > c64
    eye64 = jnp.where(r64 == c64, 1.0, 0.0)
    blk = {n: jnp.right_shift(r64, n.bit_length() - 1) == jnp.right_shift(c64, n.bit_length() - 1)
           for n in (8, 16, 32, 64)}
    onorm = on_ref[...]

    def head_cols(h, base=0):
        return slice(base + h * DN_HEAD_DIM, base + (h + 1) * DN_HEAD_DIM)

    def conv_act(c, cols, norm_scale):
        r0 = c * CHUNK + CONV_HALO
        w = cw_ref[:, cols]
        acc = xbuf[r0:r0 + CHUNK, cols] * w[CONV_WIDTH - 1:CONV_WIDTH, :]
        for j in range(CONV_WIDTH - 1):
            off = r0 - (CONV_WIDTH - 1) + j
            acc = acc + xbuf[off:off + CHUNK, cols] * w[j:j + 1, :]
        y = acc * _sigmoid(acc)
        if norm_scale is None:
            return y
        return y * (lax.rsqrt(jnp.sum(y * y, axis=-1, keepdims=True) + EPS) * norm_scale)

    def solve_chunks(cs):
        units = [(c, h) for c in cs for h in heads]
        gcc = {c: gc[c * CHUNK:(c + 1) * CHUNK, :] for c in cs}
        bet = {c: beta[c * CHUNK:(c + 1) * CHUNK, :] for c in cs}
        egc = {c: jnp.exp(gcc[c]) for c in cs}
        ekd = {c: jnp.exp(gcc[c][CHUNK - 1:CHUNK, :] - gcc[c]) for c in cs}
        q = {(c, h): conv_act(c, head_cols(h), DN_HEAD_DIM ** -0.5) for c, h in units}
        k = {(c, h): conv_act(c, head_cols(h, DN_WIDTH), 1.0) for c, h in units}
        v = {(c, h): conv_act(c, head_cols(h, 2 * DN_WIDTH), None) for c, h in units}
        bcol = {(c, h): bet[c][:, h:h + 1] for c, h in units}
        ecol = {(c, h): egc[c][:, h:h + 1] for c, h in units}
        kb = {u: k[u] * bcol[u] for u in units}
        k16 = {u: k[u].astype(BF16) for u in units}
        q16 = {u: q[u].astype(BF16) for u in units}
        dec = {(c, h): jnp.where(causal, jnp.exp(gcc[c][:, h:h + 1] - gct[c][h:h + 1, :]), 0.0) for c, h in units}
        kk = {u: _dot_nt(kb[u].astype(BF16), k16[u]) for u in units}
        qk = {u: _dot_nt(q16[u], k16[u]) for u in units}
        a_mat = {u: jnp.where(strict, kk[u] * dec[u], 0.0) for u in units}
        a_diag = {u: jnp.where(blk[8], a_mat[u], 0.0) for u in units}
        ad = {u: _split2(a_diag[u]) for u in units}
        a2s = {u: _split2(_dot_hp(ad[u], ad[u])) for u in units}
        tinv = {u: eye64 - a_diag[u] for u in units}
        tinv = {u: tinv[u] + _dot_hp(_split2(tinv[u]), a2s[u]) for u in units}
        a4s = {u: _split2(_dot_hp(a2s[u], a2s[u])) for u in units}
        tinv = {u: tinv[u] + _dot_hp(_split2(tinv[u]), a4s[u]) for u in units}
        for n in (8, 16, 32):
            ts = {u: _split2(tinv[u]) for u in units}
            low = {u: _split2(jnp.where(blk[2 * n], jnp.where(blk[n], 0.0, a_mat[u]), 0.0)) for u in units}
            lx = {u: _split2(_dot_hp(low[u], ts[u])) for u in units}
            tinv = {u: tinv[u] - _dot_hp(ts[u], lx[u]) for u in units}
        ts = {u: _split2(tinv[u]) for u in units}
        rhs = {u: _split2(jnp.concatenate([v[u] * bcol[u], kb[u] * ecol[u]], axis=1)) for u in units}
        uw = {u: _dot_hp(ts[u], rhs[u]) for u in units}
        return dict(
            u={u: uw[u][:, 0:DN_HEAD_DIM] for u in units},
            w={u: uw[u][:, DN_HEAD_DIM:2 * DN_HEAD_DIM].astype(BF16) for u in units},
            qk={u: (qk[u] * dec[u]).astype(BF16) for u in units},
            qd={u: (q[u] * ecol[u]).astype(BF16) for u in units},
            kd={(c, h): (k[c, h] * ekd[c][:, h:h + 1]).astype(BF16) for c, h in units},
        )

    def scan_chunk(c, sol):
        rows = slice(c * CHUNK, (c + 1) * CHUNK)
        egl = jnp.exp(gc[(c + 1) * CHUNK - 1:(c + 1) * CHUNK, :])
        st = [state[h] for h in heads]
        st16 = [st[h].astype(BF16) for h in heads]
        ws = [_dot(sol["w"][c, h], st16[h]) for h in heads]
        vn16 = [(sol["u"][c, h] - ws[h]).astype(BF16) for h in heads]
        o = [_dot(sol["qd"][c, h], st16[h]) + _dot(sol["qk"][c, h], vn16[h]) for h in heads]
        upd = [_dot_tn(sol["kd"][c, h], vn16[h]) for h in heads]
        for h in heads:
            state[h] = st[h] * egl[:, h:h + 1] + upd[h]
            zh = z_ref[0, rows, head_cols(h)]
            y = _rmsnorm(o[h], onorm) * (zh * _sigmoid(zh))
            y_ref[0, rows, head_cols(h)] = y.astype(y_ref.dtype)

    groups = [range(c0, c0 + SOLVE_GROUP) for c0 in range(0, nchunk, SOLVE_GROUP)]
    sol = solve_chunks(groups[0])
    for prev, cur in zip(groups[:-1], groups[1:]):
        nxt = solve_chunks(cur)
        for c in prev:
            scan_chunk(c, sol)
        sol = nxt
    for c in groups[-1]:
        scan_chunk(c, sol)

    xbuf[0:CONV_HALO, :] = xbuf[tc:tc + CONV_HALO, :]


def _deltanet(proj, ab, conv_w, alog, dtb, onorm, *, tc=256):
    b, s, _ = proj.shape
    qkv_w = 3 * DN_WIDTH
    return pl.pallas_call(
        functools.partial(_dn_kernel, tc=tc),
        grid=(b, s // tc),
        in_specs=[
            pl.BlockSpec((1, tc, qkv_w), lambda i, j: (i, j, 0)),
            pl.BlockSpec((1, tc, DN_WIDTH), lambda i, j: (i, j, qkv_w // DN_WIDTH)),
            pl.BlockSpec((1, tc, 2 * LANES), lambda i, j: (i, j, 0)),
            pl.BlockSpec((CONV_WIDTH, qkv_w), lambda i, j: (0, 0)),
            pl.BlockSpec((1, LANES), lambda i, j: (0, 0)),
            pl.BlockSpec((1, LANES), lambda i, j: (0, 0)),
            pl.BlockSpec((1, DN_HEAD_DIM), lambda i, j: (0, 0)),
        ],
        out_specs=pl.BlockSpec((1, tc, DN_WIDTH), lambda i, j: (i, j, 0)),
        out_shape=jax.ShapeDtypeStruct((b, s, DN_WIDTH), BF16),
        scratch_shapes=[
            pltpu.VMEM((tc + CONV_HALO, qkv_w), F32),
            pltpu.VMEM((DN_HEADS, DN_HEAD_DIM, DN_HEAD_DIM), F32),
        ],
        compiler_params=_cparams("parallel", "arbitrary"),
        name="deltanet",
    )(proj, proj, ab, conv_w, alog, dtb, onorm)


def _pool_kernel(p_ref, pw_ref, ps_ref, y_ref, pbuf, *, tp):
    s = pl.program_id(1)

    @pl.when(s == 0)
    def _():
        pbuf[0:POOL_HALO, :] = jnp.zeros((POOL_HALO, POOL_WIDTH), F32)

    pbuf[POOL_HALO:POOL_HALO + tp, :] = p_ref[0]
    pos1 = s * tp + lax.broadcasted_iota(jnp.int32, (tp, 1), 0) + 1
    for gi, win in enumerate(POOL_WINDOWS):
        cols = slice(gi * POOL_GROUP_DIM, (gi + 1) * POOL_GROUP_DIM)
        ext = pbuf[:, cols]
        acc = ext
        shift = 1
        while shift < win:
            acc = acc + pltpu.roll(acc, shift, 0)
            shift *= 2
        cnt = jnp.minimum(pos1, win).astype(F32)
        pooled = acc[POOL_HALO:, :] / cnt - ext[POOL_HALO:, :]
        mixed = _dot(pooled.astype(BF16), pw_ref[gi])
        y_ref[0, :, cols] = (mixed * ps_ref[:, cols]).astype(y_ref.dtype)
    pbuf[0:POOL_HALO, :] = pbuf[tp:tp + POOL_HALO, :]


def _pool(proj, pool_w, pool_scale, layer, *, tp=512):
    b, s, n = proj.shape
    ngroups = len(POOL_WINDOWS)
    return pl.pallas_call(
        functools.partial(_pool_kernel, tp=tp),
        grid=(b, s // tp),
        in_specs=[
            pl.BlockSpec((1, tp, POOL_WIDTH), lambda i, j: (i, j, n // POOL_WIDTH - 1)),
            pl.BlockSpec((None, ngroups, POOL_GROUP_DIM, POOL_GROUP_DIM), lambda i, j: (layer, 0, 0, 0)),
            pl.BlockSpec((1, POOL_WIDTH), lambda i, j: (0, 0)),
        ],
        out_specs=pl.BlockSpec((1, tp, POOL_WIDTH), lambda i, j: (i, j, 0)),
        out_shape=jax.ShapeDtypeStruct((b, s, POOL_WIDTH), BF16),
        scratch_shapes=[pltpu.VMEM((tp + POOL_HALO, POOL_WIDTH), F32)],
        compiler_params=_cparams("parallel", "arbitrary"),
        name="pool",
    )(proj, pool_w, pool_scale)


def _outproj_kernel(h_ref, y1_ref, y2_ref, w1_ref, w2_ref, o_ref):
    o_ref[...] = h_ref[...] + (_dot(y1_ref[...], w1_ref[...]) + _dot(y2_ref[...], w2_ref[...]))


def _outproj(h, y1, y2, w, layer, *, tm=512):
    t, d = h.shape
    k1, k2 = y1.shape[1], y2.shape[1]
    assert k1 == k2 and k1 + k2 == w.shape[1]
    return pl.pallas_call(
        _outproj_kernel,
        grid=(t // tm,),
        in_specs=[
            pl.BlockSpec((tm, d), lambda i: (i, 0)),
            pl.BlockSpec((tm, k1), lambda i: (i, 0)),
            pl.BlockSpec((tm, k2), lambda i: (i, 0)),
            pl.BlockSpec((None, k1, d), lambda i: (layer, 0, 0)),
            pl.BlockSpec((None, k2, d), lambda i: (layer, 1, 0)),
        ],
        out_specs=pl.BlockSpec((tm, d), lambda i: (i, 0)),
        out_shape=jax.ShapeDtypeStruct((t, d), F32),
        compiler_params=_cparams("parallel"),
        name="outproj",
    )(h, y1, y2, w, w)


def _xattn_kernel(h_ref, g_ref, wq_ref, kv_ref, wo_ref, o_ref):
    h = h_ref[0]
    hn = _rmsnorm(h, g_ref[...]).astype(BF16)
    q = _dot(hn, wq_ref[...]).astype(BF16)
    outs = []
    for hd in range(X_HEADS):
        qh = q[:, hd * X_HEAD_DIM:(hd + 1) * X_HEAD_DIM]
        kh = kv_ref[0, :, hd * X_HEAD_DIM:(hd + 1) * X_HEAD_DIM]
        vh = kv_ref[0, :, D_MODEL + hd * X_HEAD_DIM:D_MODEL + (hd + 1) * X_HEAD_DIM]
        sc = _dot_nt(qh, kh) * (X_HEAD_DIM ** -0.5)
        sc = sc - jnp.max(sc, axis=-1, keepdims=True)
        e = jnp.exp(sc)
        p = e / jnp.sum(e, axis=-1, keepdims=True)
        outs.append(_dot(p.astype(BF16), vh).astype(BF16))
    o = jnp.concatenate(outs, axis=1)
    o_ref[0] = h + _dot(o, wo_ref[...])


def _xattn(h, g, wq, kv, wo, layer, *, tm=512):
    b, s, d = h.shape
    return pl.pallas_call(
        _xattn_kernel,
        grid=(b, s // tm),
        in_specs=[
            pl.BlockSpec((1, tm, d), lambda i, j: (i, j, 0)),
            pl.BlockSpec((1, d), lambda i, j: (0, 0)),
            pl.BlockSpec((None, d, d), lambda i, j: (layer, 0, 0), pipeline_mode=pl.Buffered(1)),
            pl.BlockSpec((1, MEM_LEN, 2 * d), lambda i, j: (i, 0, 0)),
            pl.BlockSpec((None, d, d), lambda i, j: (layer, 0, 0), pipeline_mode=pl.Buffered(1)),
        ],
        out_specs=pl.BlockSpec((1, tm, d), lambda i, j: (i, j, 0)),
        out_shape=jax.ShapeDtypeStruct((b, s, d), F32),
        compiler_params=_cparams("parallel", "parallel"),
        name="xattn",
    )(h, g, wq, kv, wo)


def kernel(x, mem, ffn1_norm, ffn1_w_gate, ffn1_w_up, ffn1_w_down, mix_norm, w_in, conv_w, a_log, dt_bias, dn_out_norm, pool_w, pool_scale, w_out, xattn_norm, mem_norm, xattn_wq, xattn_wkv, xattn_wo, ffn2_norm, ffn2_w_gate, ffn2_w_up, ffn2_w_down, final_norm):
    bsz, seq, d = x.shape
    depth = w_in.shape[0]
    t = bsz * seq
    o_a = 4 * DN_WIDTH
    o_b = o_a + DN_HEADS
    o_p = o_b + DN_HEADS

    def row(v):
        return v.reshape(1, -1).astype(F32)

    def pad_lanes(v):
        return jnp.pad(v.astype(F32), (0, LANES - v.shape[0])).reshape(1, LANES)

    f1g, f1u, f1d = _to_bf16(ffn1_w_gate), _to_bf16(ffn1_w_up), _to_bf16(ffn1_w_down)
    f2g, f2u, f2d = _to_bf16(ffn2_w_gate), _to_bf16(ffn2_w_up), _to_bf16(ffn2_w_down)
    wq16, wkv16, wo16 = _to_bf16(xattn_wq), _to_bf16(xattn_wkv), _to_bf16(xattn_wo)
    wout16, poolw16 = _to_bf16(w_out), _to_bf16(pool_w)
    win16 = _to_bf16(w_in)
    w_pool16 = win16[:, :, o_p:]
    w_ab = jnp.zeros((depth, d, 2 * LANES), BF16)
    w_ab = w_ab.at[:, :, 0:DN_HEADS].set(win16[:, :, o_a:o_b])
    w_ab = w_ab.at[:, :, LANES:LANES + DN_HEADS].set(win16[:, :, o_b:o_p])

    xs = x.reshape(t, d)
    mem2 = mem.reshape(bsz * MEM_LEN, d)
    fg = row(final_norm)
    for l in range(depth):
        hs = _ffn(xs, row(ffn1_norm[l]), f1g, f1u, f1d, fg, l, final_norm=False)

        proj, ab = _inproj(hs, row(mix_norm[l]), win16, w_pool16, w_ab, l, n_main_cols=o_a)
        proj3 = proj.reshape(bsz, seq, -1)
        y_dn = _deltanet(proj3, ab.reshape(bsz, seq, -1), conv_w[l].astype(F32), pad_lanes(a_log[l]),
                         pad_lanes(dt_bias[l]), row(dn_out_norm[l]))
        y_pool = _pool(proj3, poolw16, row(pool_scale[l]), l)
        hs = _outproj(hs, y_dn.reshape(t, -1), y_pool.reshape(t, -1), wout16, l)

        kv = _norm_matmul(mem2, row(mem_norm[l]), wkv16, l, BF16, tm=512, tn=1024)
        hs = _xattn(hs.reshape(bsz, seq, d), row(xattn_norm[l]), wq16, kv.reshape(bsz, MEM_LEN, 2 * d),
                    wo16, l).reshape(t, d)

        last = l == depth - 1
        xs = _ffn(hs, row(ffn2_norm[l]), f2g, f2u, f2d, fg, l, final_norm=last, tm=512 if last else 1024)
    return xs.reshape(bsz, seq, d)
```

```python
import functools

import jax
import jax.numpy as jnp
from jax import lax
from jax.experimental import pallas as pl
from jax.experimental.pallas import tpu as pltpu

F32 = jnp.float32
BF16 = jnp.bfloat16

D_MODEL = 2048
MEM_LEN = 256
DN_HEADS = 8
DN_HEAD_DIM = 128
DN_WIDTH = DN_HEADS * DN_HEAD_DIM
POOL_WINDOWS = (2, 4, 8, 16)
POOL_WIDTH = D_MODEL - DN_WIDTH
POOL_GROUP_DIM = POOL_WIDTH // len(POOL_WINDOWS)
POOL_HALO = 16
CONV_WIDTH = 4
CONV_HALO = 8
CHUNK = 64
SOLVE_GROUP = 2
X_HEADS = 4
X_HEAD_DIM = D_MODEL // X_HEADS
EPS = 1e-6
LANES = 128
ROW_SLAB = 256
VMEM_LIMIT = 60 * 1024 * 1024


VMEM_LIMIT_FINAL_FFN = 63 * 1024 * 1024


def _cparams(*sem, vmem_limit=VMEM_LIMIT):
    return pltpu.CompilerParams(dimension_semantics=sem, vmem_limit_bytes=vmem_limit)


def _rmsnorm(x, g):
    ms = jnp.mean(x * x, axis=-1, keepdims=True)
    return x * lax.rsqrt(ms + EPS) * g


def _sigmoid(x):
    return 1.0 / (1.0 + jnp.exp(-x))


def _dot(a, b):
    return jnp.dot(a, b, preferred_element_type=F32)


def _dot_nt(a, b):
    return lax.dot_general(a, b, (((1,), (1,)), ((), ())), preferred_element_type=F32)


def _dot_tn(a, b):
    return lax.dot_general(a, b, (((0,), (0,)), ((), ())), preferred_element_type=F32)


def _split2(x):
    hi = x.astype(BF16)
    lo = (x - hi.astype(F32)).astype(BF16)
    return hi, lo


def _split3(x):
    hi = x.astype(BF16)
    r1 = x - hi.astype(F32)
    mid = r1.astype(BF16)
    lo = (r1 - mid.astype(F32)).astype(BF16)
    return hi, mid, lo


def _dot_hp(a, b):
    (ah, al), (bh, bl) = a, b
    return _dot(ah, bh) + (_dot(ah, bl) + _dot(al, bh))


def _cast_kernel(x_ref, o_ref):
    o_ref[...] = x_ref[...].astype(o_ref.dtype)


def _to_bf16(w, *, tr=512):
    shape = w.shape
    w2 = w.reshape(-1, shape[-1])
    r, c = w2.shape
    out = pl.pallas_call(
        _cast_kernel,
        grid=(r // tr,),
        in_specs=[pl.BlockSpec((tr, c), lambda i: (i, 0))],
        out_specs=pl.BlockSpec((tr, c), lambda i: (i, 0)),
        out_shape=jax.ShapeDtypeStruct((r, c), BF16),
        compiler_params=_cparams("parallel"),
        name="to_bf16",
    )(w2)
    return out.reshape(shape)


def _ffn_kernel(x_ref, g_ref, wg_ref, wu_ref, wd_ref, fg_ref, o_ref, xn_ref, *, final_norm):
    j = pl.program_id(1)

    @pl.when(j == 0)
    def _():
        for r in range(0, x_ref.shape[0], ROW_SLAB):
            rows = slice(r, r + ROW_SLAB)
            xn_ref[rows, :] = _rmsnorm(x_ref[rows, :], g_ref[...]).astype(BF16)
        o_ref[...] = jnp.zeros_like(o_ref)

    xn = xn_ref[...]
    gate = _dot(xn, wg_ref[...])
    up = _dot(xn, wu_ref[...])
    hid = (gate * _sigmoid(gate) * up).astype(BF16)
    o_ref[...] += _dot(hid, wd_ref[...])

    @pl.when(j == pl.num_programs(1) - 1)
    def _():
        for r in range(0, x_ref.shape[0], ROW_SLAB):
            rows = slice(r, r + ROW_SLAB)
            y = x_ref[rows, :] + 0.5 * o_ref[rows, :]
            if final_norm:
                y = _rmsnorm(y, fg_ref[...])
            o_ref[rows, :] = y


def _ffn(x, g, wg, wu, wd, fg, layer, *, final_norm, tm=1024, tf=512):
    t, d = x.shape
    dff = wg.shape[2]
    return pl.pallas_call(
        functools.partial(_ffn_kernel, final_norm=final_norm),
        grid=(t // tm, dff // tf),
        in_specs=[
            pl.BlockSpec((tm, d), lambda i, j: (i, 0)),
            pl.BlockSpec((1, d), lambda i, j: (0, 0)),
            pl.BlockSpec((None, d, tf), lambda i, j: (layer, 0, j)),
            pl.BlockSpec((None, d, tf), lambda i, j: (layer, 0, j)),
            pl.BlockSpec((None, tf, d), lambda i, j: (layer, j, 0)),
            pl.BlockSpec((1, d), lambda i, j: (0, 0)),
        ],
        out_specs=pl.BlockSpec((tm, d), lambda i, j: (i, 0)),
        out_shape=jax.ShapeDtypeStruct((t, d), F32),
        scratch_shapes=[pltpu.VMEM((tm, d), BF16)],
        compiler_params=_cparams("parallel", "arbitrary",
                                 vmem_limit=VMEM_LIMIT_FINAL_FFN if final_norm else VMEM_LIMIT),
        name="ffn",
    )(x, g, wg, wu, wd, fg)


def _nmm_kernel(x_ref, g_ref, w_ref, o_ref, xn_ref):
    @pl.when(pl.program_id(1) == 0)
    def _():
        xn_ref[...] = _rmsnorm(x_ref[...], g_ref[...]).astype(BF16)

    o_ref[...] = _dot(xn_ref[...], w_ref[...]).astype(o_ref.dtype)


def _norm_matmul(x, g, w, layer, out_dtype, *, tm, tn):
    t, d = x.shape
    n = w.shape[2]
    return pl.pallas_call(
        _nmm_kernel,
        grid=(t // tm, n // tn),
        in_specs=[
            pl.BlockSpec((tm, d), lambda i, j: (i, 0)),
            pl.BlockSpec((1, d), lambda i, j: (0, 0)),
            pl.BlockSpec((None, d, tn), lambda i, j: (layer, 0, j)),
        ],
        out_specs=pl.BlockSpec((tm, tn), lambda i, j: (i, j)),
        out_shape=jax.ShapeDtypeStruct((t, n), out_dtype),
        scratch_shapes=[pltpu.VMEM((tm, d), BF16)],
        compiler_params=_cparams("parallel", "arbitrary"),
        name="norm_matmul",
    )(x, g, w)


def _inproj_kernel(x_ref, g_ref, wa_ref, wp_ref, wab_ref, o_ref, ab_ref, xn_ref, *, n_main):
    j = pl.program_id(1)

    @pl.when(j == 0)
    def _():
        xn = _rmsnorm(x_ref[...], g_ref[...]).astype(BF16)
        xn_ref[...] = xn
        ab_ref[...] = _dot(xn, wab_ref[...])

    @pl.when(j < n_main)
    def _():
        o_ref[...] = _dot(xn_ref[...], wa_ref[...])

    @pl.when(j >= n_main)
    def _():
        o_ref[...] = _dot(xn_ref[...], wp_ref[...])


def _inproj(x, g, w_in, w_pool, wab, layer, *, n_main_cols, tm=1024, tn=1024):
    t, d = x.shape
    assert w_pool.shape[2] == tn and n_main_cols % tn == 0
    n_main = n_main_cols // tn
    n = n_main_cols + tn
    nab = wab.shape[2]
    return pl.pallas_call(
        functools.partial(_inproj_kernel, n_main=n_main),
        grid=(t // tm, n // tn),
        in_specs=[
            pl.BlockSpec((tm, d), lambda i, j: (i, 0)),
            pl.BlockSpec((1, d), lambda i, j: (0, 0)),
            pl.BlockSpec((None, d, tn), lambda i, j: (layer, 0, jnp.minimum(j, n_main - 1))),
            pl.BlockSpec((None, d, tn), lambda i, j: (layer, 0, 0)),
            pl.BlockSpec((None, d, nab), lambda i, j: (layer, 0, 0)),
        ],
        out_specs=[
            pl.BlockSpec((tm, tn), lambda i, j: (i, j)),
            pl.BlockSpec((tm, nab), lambda i, j: (i, 0)),
        ],
        out_shape=[jax.ShapeDtypeStruct((t, n), F32), jax.ShapeDtypeStruct((t, nab), F32)],
        scratch_shapes=[pltpu.VMEM((tm, d), BF16)],
        compiler_params=_cparams("parallel", "arbitrary"),
        name="inproj",
    )(x, g, w_in, w_pool, wab)


def _dn_kernel(qkv_ref, z_ref, ab_ref, cw_ref, alog_ref, dtb_ref, on_ref, y_ref, xbuf, state, *, tc):
    s = pl.program_id(1)
    nchunk = tc // CHUNK
    qkv_w = 3 * DN_WIDTH
    heads = range(DN_HEADS)

    @pl.when(s == 0)
    def _():
        xbuf[0:CONV_HALO, :] = jnp.zeros((CONV_HALO, qkv_w), F32)
        state[...] = jnp.zeros_like(state)

    xbuf[CONV_HALO:CONV_HALO + tc, :] = qkv_ref[0]

    ab = ab_ref[0]
    xg = ab[:, 0:LANES] + dtb_ref[...]
    softplus = jnp.maximum(xg, 0.0) + jnp.log1p(jnp.exp(-jnp.abs(xg)))
    g = -jnp.exp(alog_ref[...]) * softplus
    beta = _sigmoid(ab[:, LANES:2 * LANES])

    ri = lax.broadcasted_iota(jnp.int32, (tc, tc), 0)
    ci = lax.broadcasted_iota(jnp.int32, (tc, tc), 1)
    same_chunk = jnp.right_shift(ri, 6) == jnp.right_shift(ci, 6)
    tri = jnp.where(same_chunk, jnp.where(ci <= ri, 1.0, 0.0), 0.0).astype(BF16)
    gh, gm, gl = _split3(g)
    gc = _dot(tri, gh) + _dot(tri, gm) + _dot(tri, gl)

    er = lax.broadcasted_iota(jnp.int32, (LANES, LANES), 0)
    ec = lax.broadcasted_iota(jnp.int32, (LANES, LANES), 1)
    eye = jnp.where(er == ec, 1.0, 0.0).astype(BF16)
    gct = []
    for c in range(nchunk):
        ch, cm, cl = _split3(gc[c * CHUNK:(c + 1) * CHUNK, :])
        t = _dot_nt(eye, ch) + _dot_nt(eye, cm) + _dot_nt(eye, cl)
        gct.append(t[0:DN_HEADS, :])

    r64 = lax.broadcasted_iota(jnp.int32, (CHUNK, CHUNK), 0)
    c64 = lax.broadcasted_iota(jnp.int32, (CHUNK, CHUNK), 1)
    causal = r64 >= c64
    strict = r64 > c64
    eye64 = jnp.where(r64 == c64, 1.0, 0.0)
    blk = {n: jnp.right_shift(r64, n.bit_length() - 1) == jnp.right_shift(c64, n.bit_length() - 1)
           for n in (8, 16, 32, 64)}
    onorm = on_ref[...]

    def head_cols(h, base=0):
        return slice(base + h * DN_HEAD_DIM, base + (h + 1) * DN_HEAD_DIM)

    def conv_act(c, cols, norm_scale):
        r0 = c * CHUNK + CONV_HALO
        w = cw_ref[:, cols]
        acc = xbuf[r0:r0 + CHUNK, cols] * w[CONV_WIDTH - 1:CONV_WIDTH, :]
        for j in range(CONV_WIDTH - 1):
            off = r0 - (CONV_WIDTH - 1) + j
            acc = acc + xbuf[off:off + CHUNK, cols] * w[j:j + 1, :]
        y = acc * _sigmoid(acc)
        if norm_scale is None:
            return y
        return y * (lax.rsqrt(jnp.sum(y * y, axis=-1, keepdims=True) + EPS) * norm_scale)

    def solve_chunks(cs):
        units = [(c, h) for c in cs for h in heads]
        gcc = {c: gc[c * CHUNK:(c + 1) * CHUNK, :] for c in cs}
        bet = {c: beta[c * CHUNK:(c + 1) * CHUNK, :] for c in cs}
        egc = {c: jnp.exp(gcc[c]) for c in cs}
        ekd = {c: jnp.exp(gcc[c][CHUNK - 1:CHUNK, :] - gcc[c]) for c in cs}
        q = {(c, h): conv_act(c, head_cols(h), DN_HEAD_DIM ** -0.5) for c, h in units}
        k = {(c, h): conv_act(c, head_cols(h, DN_WIDTH), 1.0) for c, h in units}
        v = {(c, h): conv_act(c, head_cols(h, 2 * DN_WIDTH), None) for c, h in units}
        bcol = {(c, h): bet[c][:, h:h + 1] for c, h in units}
        ecol = {(c, h): egc[c][:, h:h + 1] for c, h in units}
        kb = {u: k[u] * bcol[u] for u in units}
        k16 = {u: k[u].astype(BF16) for u in units}
        q16 = {u: q[u].astype(BF16) for u in units}
        dec = {(c, h): jnp.where(causal, jnp.exp(gcc[c][:, h:h + 1] - gct[c][h:h + 1, :]), 0.0) for c, h in units}
        kk = {u: _dot_nt(kb[u].astype(BF16), k16[u]) for u in units}
        qk = {u: _dot_nt(q16[u], k16[u]) for u in units}
        a_mat = {u: jnp.where(strict, kk[u] * dec[u], 0.0) for u in units}
        a_diag = {u: jnp.where(blk[8], a_mat[u], 0.0) for u in units}
        ad = {u: _split2(a_diag[u]) for u in units}
        a2s = {u: _split2(_dot_hp(ad[u], ad[u])) for u in units}
        tinv = {u: eye64 - a_diag[u] for u in units}
        tinv = {u: tinv[u] + _dot_hp(_split2(tinv[u]), a2s[u]) for u in units}
        a4s = {u: _split2(_dot_hp(a2s[u], a2s[u])) for u in units}
        tinv = {u: tinv[u] + _dot_hp(_split2(tinv[u]), a4s[u]) for u in units}
        for n in (8, 16, 32):
            ts = {u: _split2(tinv[u]) for u in units}
            low = {u: _split2(jnp.where(blk[2 * n], jnp.where(blk[n], 0.0, a_mat[u]), 0.0)) for u in units}
            lx = {u: _split2(_dot_hp(low[u], ts[u])) for u in units}
            tinv = {u: tinv[u] - _dot_hp(ts[u], lx[u]) for u in units}
        ts = {u: _split2(tinv[u]) for u in units}
        rhs = {u: _split2(jnp.concatenate([v[u] * bcol[u], kb[u] * ecol[u]], axis=1)) for u in units}
        uw = {u: _dot_hp(ts[u], rhs[u]) for u in units}
        return dict(
            u={u: uw[u][:, 0:DN_HEAD_DIM] for u in units},
            w={u: uw[u][:, DN_HEAD_DIM:2 * DN_HEAD_DIM].astype(BF16) for u in units},
            qk={u: (qk[u] * dec[u]).astype(BF16) for u in units},
            qd={u: (q[u] * ecol[u]).astype(BF16) for u in units},
            kd={(c, h): (k[c, h] * ekd[c][:, h:h + 1]).astype(BF16) for c, h in units},
        )

    def scan_chunk(c, sol):
        rows = slice(c * CHUNK, (c + 1) * CHUNK)
        egl = jnp.exp(gc[(c + 1) * CHUNK - 1:(c + 1) * CHUNK, :])
        st = [state[h] for h in heads]
        st16 = [st[h].astype(BF16) for h in heads]
        ws = [_dot(sol["w"][c, h], st16[h]) for h in heads]
        vn16 = [(sol["u"][c, h] - ws[h]).astype(BF16) for h in heads]
        o = [_dot(sol["qd"][c, h], st16[h]) + _dot(sol["qk"][c, h], vn16[h]) for h in heads]
        upd = [_dot_tn(sol["kd"][c, h], vn16[h]) for h in heads]
        for h in heads:
            state[h] = st[h] * egl[:, h:h + 1] + upd[h]
            zh = z_ref[0, rows, head_cols(h)]
            y = _rmsnorm(o[h], onorm) * (zh * _sigmoid(zh))
            y_ref[0, rows, head_cols(h)] = y.astype(y_ref.dtype)

    groups = [range(c0, c0 + SOLVE_GROUP) for c0 in range(0, nchunk, SOLVE_GROUP)]
    sol = solve_chunks(groups[0])
    for prev, cur in zip(groups[:-1], groups[1:]):
        nxt = solve_chunks(cur)
        for c in prev:
            scan_chunk(c, sol)
        sol = nxt
    for c in groups[-1]:
        scan_chunk(c, sol)

    xbuf[0:CONV_HALO, :] = xbuf[tc:tc + CONV_HALO, :]


def _deltanet(proj, ab, conv_w, alog, dtb, onorm, *, tc=256):
    b, s, _ = proj.shape
    qkv_w = 3 * DN_WIDTH
    return pl.pallas_call(
        functools.partial(_dn_kernel, tc=tc),
        grid=(b, s // tc),
        in_specs=[
            pl.BlockSpec((1, tc, qkv_w), lambda i, j: (i, j, 0)),
            pl.BlockSpec((1, tc, DN_WIDTH), lambda i, j: (i, j, qkv_w // DN_WIDTH)),
            pl.BlockSpec((1, tc, 2 * LANES), lambda i, j: (i, j, 0)),
            pl.BlockSpec((CONV_WIDTH, qkv_w), lambda i, j: (0, 0)),
            pl.BlockSpec((1, LANES), lambda i, j: (0, 0)),
            pl.BlockSpec((1, LANES), lambda i, j: (0, 0)),
            pl.BlockSpec((1, DN_HEAD_DIM), lambda i, j: (0, 0)),
        ],
        out_specs=pl.BlockSpec((1, tc, DN_WIDTH), lambda i, j: (i, j, 0)),
        out_shape=jax.ShapeDtypeStruct((b, s, DN_WIDTH), BF16),
        scratch_shapes=[
            pltpu.VMEM((tc + CONV_HALO, qkv_w), F32),
            pltpu.VMEM((DN_HEADS, DN_HEAD_DIM, DN_HEAD_DIM), F32),
        ],
        compiler_params=_cparams("parallel", "arbitrary"),
        name="deltanet",
    )(proj, proj, ab, conv_w, alog, dtb, onorm)


def _pool_kernel(p_ref, pw_ref, ps_ref, y_ref, pbuf, *, tp):
    s = pl.program_id(1)

    @pl.when(s == 0)
    def _():
        pbuf[0:POOL_HALO, :] = jnp.zeros((POOL_HALO, POOL_WIDTH), F32)

    pbuf[POOL_HALO:POOL_HALO + tp, :] = p_ref[0]
    pos1 = s * tp + lax.broadcasted_iota(jnp.int32, (tp, 1), 0) + 1
    for gi, win in enumerate(POOL_WINDOWS):
        cols = slice(gi * POOL_GROUP_DIM, (gi + 1) * POOL_GROUP_DIM)
        ext = pbuf[:, cols]
        acc = ext
        shift = 1
        while shift < win:
            acc = acc + pltpu.roll(acc, shift, 0)
            shift *= 2
        cnt = jnp.minimum(pos1, win).astype(F32)
        pooled = acc[POOL_HALO:, :] / cnt - ext[POOL_HALO:, :]
        mixed = _dot(pooled.astype(BF16), pw_ref[gi])
        y_ref[0, :, cols] = (mixed * ps_ref[:, cols]).astype(y_ref.dtype)
    pbuf[0:POOL_HALO, :] = pbuf[tp:tp + POOL_HALO, :]


def _pool(proj, pool_w, pool_scale, layer, *, tp=512):
    b, s, n = proj.shape
    ngroups = len(POOL_WINDOWS)
    return pl.pallas_call(
        functools.partial(_pool_kernel, tp=tp),
        grid=(b, s // tp),
        in_specs=[
            pl.BlockSpec((1, tp, POOL_WIDTH), lambda i, j: (i, j, n // POOL_WIDTH - 1)),
            pl.BlockSpec((None, ngroups, POOL_GROUP_DIM, POOL_GROUP_DIM), lambda i, j: (layer, 0, 0, 0)),
            pl.BlockSpec((1, POOL_WIDTH), lambda i, j: (0, 0)),
        ],
        out_specs=pl.BlockSpec((1, tp, POOL_WIDTH), lambda i, j: (i, j, 0)),
        out_shape=jax.ShapeDtypeStruct((b, s, POOL_WIDTH), BF16),
        scratch_shapes=[pltpu.VMEM((tp + POOL_HALO, POOL_WIDTH), F32)],
        compiler_params=_cparams("parallel", "arbitrary"),
        name="pool",
    )(proj, pool_w, pool_scale)


def _outproj_kernel(h_ref, y1_ref, y2_ref, w1_ref, w2_ref, o_ref):
    o_ref[...] = h_ref[...] + (_dot(y1_ref[...], w1_ref[...]) + _dot(y2_ref[...], w2_ref[...]))


def _outproj(h, y1, y2, w, layer, *, tm=512):
    t, d = h.shape
    k1, k2 = y1.shape[1], y2.shape[1]
    assert k1 == k2 and k1 + k2 == w.shape[1]
    return pl.pallas_call(
        _outproj_kernel,
        grid=(t // tm,),
        in_specs=[
            pl.BlockSpec((tm, d), lambda i: (i, 0)),
            pl.BlockSpec((tm, k1), lambda i: (i, 0)),
            pl.BlockSpec((tm, k2), lambda i: (i, 0)),
            pl.BlockSpec((None, k1, d), lambda i: (layer, 0, 0)),
            pl.BlockSpec((None, k2, d), lambda i: (layer, 1, 0)),
        ],
        out_specs=pl.BlockSpec((tm, d), lambda i: (i, 0)),
        out_shape=jax.ShapeDtypeStruct((t, d), F32),
        compiler_params=_cparams("parallel"),
        name="outproj",
    )(h, y1, y2, w, w)


def _xattn_kernel(h_ref, g_ref, wq_ref, kv_ref, wo_ref, o_ref):
    h = h_ref[0]
    hn = _rmsnorm(h, g_ref[...]).astype(BF16)
    q = _dot(hn, wq_ref[...]).astype(BF16)
    outs = []
    for hd in range(X_HEADS):
        qh = q[:, hd * X_HEAD_DIM:(hd + 1) * X_HEAD_DIM]
        kh = kv_ref[0, :, hd * X_HEAD_DIM:(hd + 1) * X_HEAD_DIM]
        vh = kv_ref[0, :, D_MODEL + hd * X_HEAD_DIM:D_MODEL + (hd + 1) * X_HEAD_DIM]
        sc = _dot_nt(qh, kh) * (X_HEAD_DIM ** -0.5)
        sc = sc - jnp.max(sc, axis=-1, keepdims=True)
        e = jnp.exp(sc)
        p = e / jnp.sum(e, axis=-1, keepdims=True)
        outs.append(_dot(p.astype(BF16), vh).astype(BF16))
    o = jnp.concatenate(outs, axis=1)
    o_ref[0] = h + _dot(o, wo_ref[...])


def _xattn(h, g, wq, kv, wo, layer, *, tm=512):
    b, s, d = h.shape
    return pl.pallas_call(
        _xattn_kernel,
        grid=(b, s // tm),
        in_specs=[
            pl.BlockSpec((1, tm, d), lambda i, j: (i, j, 0)),
            pl.BlockSpec((1, d), lambda i, j: (0, 0)),
            pl.BlockSpec((None, d, d), lambda i, j: (layer, 0, 0), pipeline_mode=pl.Buffered(1)),
            pl.BlockSpec((1, MEM_LEN, 2 * d), lambda i, j: (i, 0, 0)),
            pl.BlockSpec((None, d, d), lambda i, j: (layer, 0, 0), pipeline_mode=pl.Buffered(1)),
        ],
        out_specs=pl.BlockSpec((1, tm, d), lambda i, j: (i, j, 0)),
        out_shape=jax.ShapeDtypeStruct((b, s, d), F32),
        compiler_params=_cparams("parallel", "parallel"),
        name="xattn",
    )(h, g, wq, kv, wo)


def kernel(x, mem, ffn1_norm, ffn1_w_gate, ffn1_w_up, ffn1_w_down, mix_norm, w_in, conv_w, a_log, dt_bias, dn_out_norm, pool_w, pool_scale, w_out, xattn_norm, mem_norm, xattn_wq, xattn_wkv, xattn_wo, ffn2_norm, ffn2_w_gate, ffn2_w_up, ffn2_w_down, final_norm):
    bsz, seq, d = x.shape
    depth = w_in.shape[0]
    t = bsz * seq
    o_a = 4 * DN_WIDTH
    o_b = o_a + DN_HEADS
    o_p = o_b + DN_HEADS

    def row(v):
        return v.reshape(1, -1).astype(F32)

    def pad_lanes(v):
        return jnp.pad(v.astype(F32), (0, LANES - v.shape[0])).reshape(1, LANES)

    f1g, f1u, f1d = _to_bf16(ffn1_w_gate), _to_bf16(ffn1_w_up), _to_bf16(ffn1_w_down)
    f2g, f2u, f2d = _to_bf16(ffn2_w_gate), _to_bf16(ffn2_w_up), _to_bf16(ffn2_w_down)
    wq16, wkv16, wo16 = _to_bf16(xattn_wq), _to_bf16(xattn_wkv), _to_bf16(xattn_wo)
    wout16, poolw16 = _to_bf16(w_out), _to_bf16(pool_w)
    win16 = _to_bf16(w_in)
    w_pool16 = win16[:, :, o_p:]
    w_ab = jnp.zeros((depth, d, 2 * LANES), BF16)
    w_ab = w_ab.at[:, :, 0:DN_HEADS].set(win16[:, :, o_a:o_b])
    w_ab = w_ab.at[:, :, LANES:LANES + DN_HEADS].set(win16[:, :, o_b:o_p])

    xs = x.reshape(t, d)
    mem2 = mem.reshape(bsz * MEM_LEN, d)
    fg = row(final_norm)
    for l in range(depth):
        hs = _ffn(xs, row(ffn1_norm[l]), f1g, f1u, f1d, fg, l, final_norm=False)

        proj, ab = _inproj(hs, row(mix_norm[l]), win16, w_pool16, w_ab, l, n_main_cols=o_a)
        proj3 = proj.reshape(bsz, seq, -1)
        y_dn = _deltanet(proj3, ab.reshape(bsz, seq, -1), conv_w[l].astype(F32), pad_lanes(a_log[l]),
                         pad_lanes(dt_bias[l]), row(dn_out_norm[l]))
        y_pool = _pool(proj3, poolw16, row(pool_scale[l]), l)
        hs = _outproj(hs, y_dn.reshape(t, -1), y_pool.reshape(t, -1), wout16, l)

        kv = _norm_matmul(mem2, row(mem_norm[l]), wkv16, l, BF16, tm=512, tn=1024)
        hs = _xattn(hs.reshape(bsz, seq, d), row(xattn_norm[l]), wq16, kv.reshape(bsz, MEM_LEN, 2 * d),
                    wo16, l).reshape(t, d)

        xs = _ffn(hs, row(ffn2_norm[l]), f2g, f2u, f2d, fg, l, final_norm=(l == depth - 1))
    return xs.reshape(bsz, seq, d)
```
